```python
import jax, jax.numpy as jnp
from jax import lax
import numpy as np

D_MODEL = 4096
BATCH = 32
SEQ = 256
DEPTH = 2
DEC_BATCH = 8
DEC_SEQ = 1024
PAST_LEN = 512

GRID_W = 64
N_MIXERS = 2
N_CONV_LAYERS = (DEPTH + 1) // 2
N_LRU_LAYERS = DEPTH // 2
CONV_WIDTH = 31
CONV_PAD = CONV_WIDTH // 2
D_CONV = D_MODEL
D_RNN = ((D_MODEL * 4 // 3) // 256) * 256
N_RNN_HEADS = 16
RNN_BLOCK = D_RNN // N_RNN_HEADS
SHORT_CONV = 4
LRU_C = 8.0
N_GROUPS = 4
EXPERTS_PER_GROUP = 8
N_EXPERTS = N_GROUPS * EXPERTS_PER_GROUP
TOP_K = 2
D_EXPERT = D_MODEL // 4
MOE_BLOCK = 128
ALPHA = (2 * DEPTH) ** 0.25
BETA = (8 * DEPTH) ** -0.25
LN_EPS = 1e-5

kernel_name = "hybrid_conv_rglru_hmoe_diffusion_step"


def layer_norm(x, g, b):
    xf = x.astype(jnp.float32)
    mu = xf.mean(-1, keepdims=True)
    var = jnp.square(xf - mu).mean(-1, keepdims=True)
    return ((xf - mu) * lax.rsqrt(var + LN_EPS)).astype(x.dtype) * g + b


def adaln(cond, w, b):
    m = jax.nn.silu(cond) @ w + b
    return jnp.split(m[:, None, :], 6, axis=-1)


def modulate(x, shift, scale):
    return x * (1.0 + scale) + shift


def depthwise_conv(u, w, pad_left, pad_right):
    return lax.conv_general_dilated(
        u, w[:, None, :].astype(u.dtype), window_strides=(1,),
        padding=[(pad_left, pad_right)],
        dimension_numbers=('NWC', 'WIO', 'NWC'),
        feature_group_count=u.shape[-1])


def conv_module(h, rows, w_in, b_in, w_dw, b_dw, g_n, b_n, w_out, b_out):
    bsz, t, _ = h.shape
    a, gt = jnp.split(h @ w_in + b_in, 2, axis=-1)
    u = a * jax.nn.sigmoid(gt)
    u = u.reshape(bsz * rows, t // rows, D_CONV)
    u = depthwise_conv(u, w_dw, CONV_PAD, CONV_PAD) + b_dw
    u = u.reshape(bsz, t, D_CONV)
    u = jax.nn.silu(layer_norm(u, g_n, b_n))
    return u @ w_out + b_out


def block_diag(x, w):
    bsz, t, _ = x.shape
    xh = x.reshape(bsz, t, N_RNN_HEADS, RNN_BLOCK)
    return jnp.einsum('bthi,hij->bthj', xh, w.astype(x.dtype)).reshape(bsz, t, D_RNN)


def linear_recurrence(a, b, h0, reverse):
    def combine(l, r):
        return (l[0] * r[0], r[0] * l[1] + r[1])
    acc_a, acc_b = lax.associative_scan(combine, (a, b), reverse=reverse, axis=1)
    return acc_a * h0[:, None, :] + acc_b


def lru_mixer(h, h0_f, h0_b, w_in, b_in, w_sc, b_sc, w_a, b_a, w_x, b_x, lam, w_out, b_out):
    y, xb = jnp.split(h @ w_in + b_in, 2, axis=-1)
    xc = depthwise_conv(xb, w_sc, 2, 1) + b_sc
    xc32 = xc.astype(jnp.float32)
    outs = []
    finals = []
    for d in range(2):
        r = jax.nn.sigmoid(block_diag(xc, w_a[d]) + b_a[d]).astype(jnp.float32)
        i = jax.nn.sigmoid(block_diag(xc, w_x[d]) + b_x[d]).astype(jnp.float32)
        log_a = -LRU_C * jax.nn.softplus(-lam[d].astype(jnp.float32)) * r
        a = jnp.exp(log_a)
        bterm = jnp.sqrt(-jnp.expm1(2.0 * log_a)) * (i * xc32)
        h0 = (h0_f if d == 0 else h0_b).astype(jnp.float32)
        hs = linear_recurrence(a, bterm, h0, reverse=(d == 1))
        outs.append(hs)
        finals.append(hs[:, -1] if d == 0 else hs[:, 0])
    mixed = (outs[0] + outs[1]).astype(h.dtype) * jax.nn.gelu(y)
    return mixed @ w_out + b_out, finals[0].astype(h.dtype), finals[1].astype(h.dtype)


def gated_mlp(xb, w1, w3, w2):
    return (jax.nn.silu(xb @ w1) * (xb @ w3)) @ w2


def hier_moe(x, rg_w, rg_b, re_w, re_b, w1, w3, w2):
    bsz, t, d = x.shape
    n = bsz * t
    nk = n * TOP_K
    xf = x.reshape(n, d)
    x32 = xf.astype(jnp.float32)
    g_prob = jax.nn.softmax(x32 @ rg_w.astype(jnp.float32) + rg_b.astype(jnp.float32), axis=-1)
    grp = jnp.argmax(g_prob, axis=-1)
    p_grp = jnp.take_along_axis(g_prob, grp[:, None], axis=1)
    e_logits = (x32 @ re_w.astype(jnp.float32) + re_b.astype(jnp.float32)).reshape(n, N_GROUPS, EXPERTS_PER_GROUP)
    e_sel = jnp.take_along_axis(e_logits, grp[:, None, None], axis=1)[:, 0]
    top_l, top_i = lax.top_k(e_sel, TOP_K)
    top_w = jax.nn.softmax(top_l, axis=-1) * p_grp
    flat_e = (grp[:, None] * EXPERTS_PER_GROUP + top_i).reshape(-1).astype(jnp.int32)
    flat_tok = jnp.repeat(jnp.arange(n, dtype=jnp.int32), TOP_K)
    flat_w = top_w.reshape(-1)
    counts = jnp.bincount(flat_e, length=N_EXPERTS).astype(jnp.int32)
    padded = ((counts + MOE_BLOCK - 1) // MOE_BLOCK) * MOE_BLOCK
    ends_pad = jnp.cumsum(padded)
    starts_pad = ends_pad - padded
    starts = jnp.cumsum(counts) - counts
    order = jnp.argsort(flat_e)
    se = flat_e[order]
    dest = starts_pad[se] + jnp.arange(nk, dtype=jnp.int32) - starts[se]
    n_blocks = -(-nk // MOE_BLOCK) + N_EXPERTS
    slots = n_blocks * MOE_BLOCK
    slot_tok = jnp.full((slots,), n, jnp.int32).at[dest].set(flat_tok[order])
    slot_w = jnp.zeros((slots,), jnp.float32).at[dest].set(flat_w[order])
    blk_start = jnp.arange(n_blocks, dtype=jnp.int32) * MOE_BLOCK
    blk_e = jnp.minimum(jnp.searchsorted(ends_pad, blk_start, side='right'), N_EXPERTS - 1)
    x_pad = jnp.concatenate([xf, jnp.zeros((1, d), xf.dtype)], axis=0)
    xs = x_pad[slot_tok].reshape(n_blocks, MOE_BLOCK, d)

    def run_block(args):
        xb, e = args
        return gated_mlp(xb, w1[e], w3[e], w2[e])

    ys = lax.map(run_block, (xs, blk_e)).reshape(slots, d)
    out = jnp.zeros((n + 1, d), x.dtype).at[slot_tok].add(ys * slot_w[:, None].astype(ys.dtype))
    return out[:n].reshape(bsz, t, d)


def setup_inputs(seed: int = 0) -> dict:
    key = jax.random.key(seed)
    ks = iter(jax.random.split(key, 48))
    D = D_MODEL

    def nrm(shape, scale):
        return jax.random.normal(next(ks), shape, jnp.float32) * scale

    x_prompt = nrm((BATCH, SEQ, D), 1.0)
    x_sample = nrm((DEC_BATCH, DEC_SEQ, D), 1.0)
    state_lru = nrm((DEC_BATCH, N_LRU_LAYERS, 2, D_RNN), 0.5)
    c = nrm((DEC_BATCH, D), 1.0)
    c_ctx = nrm((D,), 1.0)
    mod_w = nrm((DEPTH, D, 6 * D), 0.5 * D ** -0.5)
    mod_b = nrm((DEPTH, 6 * D), 0.02)
    ln_g = 1.0 + nrm((DEPTH, 2, D), 0.02)
    ln_b = nrm((DEPTH, 2, D), 0.02)
    conv_w_in = nrm((N_CONV_LAYERS, D, 2 * D_CONV), D ** -0.5)
    conv_b_in = nrm((N_CONV_LAYERS, 2 * D_CONV), 0.02)
    conv_w_dw = nrm((N_CONV_LAYERS, CONV_WIDTH, D_CONV), CONV_WIDTH ** -0.5)
    conv_b_dw = nrm((N_CONV_LAYERS, D_CONV), 0.02)
    conv_ln_g = 1.0 + nrm((N_CONV_LAYERS, D_CONV), 0.02)
    conv_ln_b = nrm((N_CONV_LAYERS, D_CONV), 0.02)
    conv_w_out = nrm((N_CONV_LAYERS, D_CONV, D), BETA * D_CONV ** -0.5)
    conv_b_out = nrm((N_CONV_LAYERS, D), 0.02)
    lru_w_in = nrm((N_LRU_LAYERS, D, 2 * D_RNN), D ** -0.5)
    lru_b_in = nrm((N_LRU_LAYERS, 2 * D_RNN), 0.02)
    lru_w_sc = nrm((N_LRU_LAYERS, SHORT_CONV, D_RNN), SHORT_CONV ** -0.5)
    lru_b_sc = nrm((N_LRU_LAYERS, D_RNN), 0.02)
    lru_w_a = nrm((N_LRU_LAYERS, 2, N_RNN_HEADS, RNN_BLOCK, RNN_BLOCK), RNN_BLOCK ** -0.5)
    lru_b_a = nrm((N_LRU_LAYERS, 2, D_RNN), 0.02)
    lru_w_x = nrm((N_LRU_LAYERS, 2, N_RNN_HEADS, RNN_BLOCK, RNN_BLOCK), RNN_BLOCK ** -0.5)
    lru_b_x = nrm((N_LRU_LAYERS, 2, D_RNN), 0.02)
    u = jax.random.uniform(next(ks), (N_LRU_LAYERS, 2, D_RNN), jnp.float32, minval=0.9, maxval=0.999)
    s = u ** (1.0 / LRU_C)
    lru_lam = jnp.log(s) - jnp.log1p(-s)
    lru_w_out = nrm((N_LRU_LAYERS, D_RNN, D), BETA * D_RNN ** -0.5)
    lru_b_out = nrm((N_LRU_LAYERS, D), 0.02)
    moe_rg_w = nrm((DEPTH, D, N_GROUPS), D ** -0.5)
    moe_rg_b = nrm((DEPTH, N_GROUPS), 0.01)
    moe_re_w = nrm((DEPTH, D, N_EXPERTS), D ** -0.5)
    moe_re_b = nrm((DEPTH, N_EXPERTS), 0.01)
    moe_w1 = nrm((DEPTH, N_EXPERTS, D, D_EXPERT), D ** -0.5)
    moe_w3 = nrm((DEPTH, N_EXPERTS, D, D_EXPERT), D ** -0.5)
    moe_w2 = nrm((DEPTH, N_EXPERTS, D_EXPERT, D), BETA * D_EXPERT ** -0.5)
    return {
        "x_prompt": x_prompt, "x_sample": x_sample, "state_lru": state_lru,
        "c": c, "c_ctx": c_ctx,
        "mod_w": mod_w, "mod_b": mod_b, "ln_g": ln_g, "ln_b": ln_b,
        "conv_w_in": conv_w_in, "conv_b_in": conv_b_in, "conv_w_dw": conv_w_dw,
        "conv_b_dw": conv_b_dw, "conv_ln_g": conv_ln_g, "conv_ln_b": conv_ln_b,
        "conv_w_out": conv_w_out, "conv_b_out": conv_b_out,
        "lru_w_in": lru_w_in, "lru_b_in": lru_b_in, "lru_w_sc": lru_w_sc, "lru_b_sc": lru_b_sc,
        "lru_w_a": lru_w_a, "lru_b_a": lru_b_a, "lru_w_x": lru_w_x, "lru_b_x": lru_b_x,
        "lru_lam": lru_lam, "lru_w_out": lru_w_out, "lru_b_out": lru_b_out,
        "moe_rg_w": moe_rg_w, "moe_rg_b": moe_rg_b, "moe_re_w": moe_re_w, "moe_re_b": moe_re_b,
        "moe_w1": moe_w1, "moe_w3": moe_w3, "moe_w2": moe_w2,
    }


def reference(x_prompt, x_sample, state_lru, c, c_ctx, mod_w, mod_b, ln_g, ln_b,
              conv_w_in, conv_b_in, conv_w_dw, conv_b_dw, conv_ln_g, conv_ln_b, conv_w_out, conv_b_out,
              lru_w_in, lru_b_in, lru_w_sc, lru_b_sc, lru_w_a, lru_b_a, lru_w_x, lru_b_x,
              lru_lam, lru_w_out, lru_b_out,
              moe_rg_w, moe_rg_b, moe_re_w, moe_re_b, moe_w1, moe_w3, moe_w2):
    rows_lat = x_sample.shape[1] // GRID_W
    xp = x_prompt
    xs = x_sample
    new_states = []
    for l in range(DEPTH):
        j = l // N_MIXERS
        sp1, cp1, gp1, sp2, cp2, gp2 = adaln(c_ctx[None, :], mod_w[l], mod_b[l])
        ss1, cs1, gs1, ss2, cs2, gs2 = adaln(c, mod_w[l], mod_b[l])
        hp = modulate(xp, sp1, cp1)
        hs = modulate(xs, ss1, cs1)
        if l % N_MIXERS == 0:
            cw = (conv_w_in[j], conv_b_in[j], conv_w_dw[j], conv_b_dw[j],
                  conv_ln_g[j], conv_ln_b[j], conv_w_out[j], conv_b_out[j])
            mp = conv_module(hp, 1, *cw)
            ms = conv_module(hs, rows_lat, *cw)
        else:
            lw = (lru_w_in[j], lru_b_in[j], lru_w_sc[j], lru_b_sc[j], lru_w_a[j], lru_b_a[j],
                  lru_w_x[j], lru_b_x[j], lru_lam[j], lru_w_out[j], lru_b_out[j])
            h_zero = jnp.zeros((xp.shape[0], D_RNN), xp.dtype)
            mp, hf, hb = lru_mixer(hp, h_zero, h_zero, *lw)
            ms, _, _ = lru_mixer(hs, state_lru[:, j, 0], state_lru[:, j, 1], *lw)
            new_states.append(jnp.stack([hf, hb], axis=1))
        xp = layer_norm(ALPHA * xp + gp1 * mp, ln_g[l, 0], ln_b[l, 0])
        xs = layer_norm(ALPHA * xs + gs1 * ms, ln_g[l, 0], ln_b[l, 0])
        mw = (moe_rg_w[l], moe_rg_b[l], moe_re_w[l], moe_re_b[l], moe_w1[l], moe_w3[l], moe_w2[l])
        fp = hier_moe(modulate(xp, sp2, cp2), *mw)
        fs = hier_moe(modulate(xs, ss2, cs2), *mw)
        xp = layer_norm(ALPHA * xp + gp2 * fp, ln_g[l, 1], ln_b[l, 1])
        xs = layer_norm(ALPHA * xs + gs2 * fs, ln_g[l, 1], ln_b[l, 1])
    new_state_lru = jnp.stack(new_states, axis=1)
    y_prompt = xp
    y_sample = xs
    return (y_prompt, y_sample, new_state_lru)
```

```python
import functools
import math

import jax
import jax.numpy as jnp
from jax import lax
from jax.experimental import pallas as pl
from jax.experimental.pallas import tpu as pltpu

F32 = jnp.float32
BF16 = jnp.bfloat16

GRID_W = 64
TOP_K = 2
LRU_C = 8.0
LN_EPS = 1e-5
LANE = 128
SUBLANE = 8
VMEM_LIMIT = 56 * 1024 * 1024
MOE_TB = 256
ROW_TILE = 256


def _sigmoid(x):
    return 1.0 / (1.0 + jnp.exp(-x))


def _round_up(x, m):
    return (x + m - 1) // m * m


def _cparams(n_axes):
    return pltpu.CompilerParams(
        dimension_semantics=("arbitrary",) * n_axes, vmem_limit_bytes=VMEM_LIMIT)


def _mod_row_fn(tile, n_prompt, dec_seq, ctx_row):
    def fn(i):
        t0 = i * tile
        return jnp.where(t0 < n_prompt, ctx_row, (t0 - n_prompt) // dec_seq)
    return fn


def _adaln_kernel(c_ref, w_ref, b_ref, o_ref):
    a = c_ref[...]
    a = (a * _sigmoid(a)).astype(BF16)
    o_ref[...] = jnp.dot(a, w_ref[...].astype(BF16), preferred_element_type=F32) + b_ref[...]


def _adaln(cond, mod_w, mod_b):
    depth, d, n6 = mod_w.shape
    r = cond.shape[0]
    tn = 512
    return pl.pallas_call(
        _adaln_kernel,
        out_shape=jax.ShapeDtypeStruct((depth, r, n6), F32),
        grid=(depth, n6 // tn),
        in_specs=[
            pl.BlockSpec((r, d), lambda l, j: (0, 0)),
            pl.BlockSpec((None, d, tn), lambda l, j: (l, 0, j)),
            pl.BlockSpec((None, 1, tn), lambda l, j: (l, 0, j)),
        ],
        out_specs=pl.BlockSpec((None, r, tn), lambda l, j: (l, 0, j)),
        compiler_params=_cparams(2),
        name="adaln",
    )(cond, mod_w, mod_b.reshape(depth, 1, n6))


def _inproj_kernel(x_ref, sh_ref, sc_ref, *rest, glu):
    if glu:
        wa_ref, wg_ref, ba_ref, bg_ref, o_ref, xs_ref = rest
    else:
        wa_ref, ba_ref, o_ref, xs_ref = rest

    @pl.when(pl.program_id(1) == 0)
    def _():
        xs_ref[...] = (x_ref[...] * (1.0 + sc_ref[...]) + sh_ref[...]).astype(BF16)

    xs = xs_ref[...]
    a = jnp.dot(xs, wa_ref[...], preferred_element_type=F32) + ba_ref[...]
    if glu:
        g = jnp.dot(xs, wg_ref[...], preferred_element_type=F32) + bg_ref[...]
        a = a * _sigmoid(g)
    o_ref[...] = a.astype(o_ref.dtype)


def _inproj(x, mods, k_shift, k_scale, w, b, mod_row, *, glu, tm=512, tn=512):
    n, d = x.shape
    nw = w.shape[1]
    nout = nw // 2 if glu else nw
    nj = nout // tn
    b2 = b.reshape(1, nw)
    row = lambda i, j: mod_row(i * (tm // ROW_TILE))
    in_specs = [
        pl.BlockSpec((tm, d), lambda i, j: (i, 0)),
        pl.BlockSpec((None, None, 1, d), lambda i, j: (k_shift, row(i, j), 0, 0)),
        pl.BlockSpec((None, None, 1, d), lambda i, j: (k_scale, row(i, j), 0, 0)),
    ]
    args = [x, mods, mods]
    if glu:
        in_specs += [
            pl.BlockSpec((d, tn), lambda i, j: (0, j)),
            pl.BlockSpec((d, tn), lambda i, j: (0, j + nj)),
            pl.BlockSpec((1, tn), lambda i, j: (0, j)),
            pl.BlockSpec((1, tn), lambda i, j: (0, j + nj)),
        ]
        args += [w, w, b2, b2]
    else:
        in_specs += [
            pl.BlockSpec((d, tn), lambda i, j: (0, j)),
            pl.BlockSpec((1, tn), lambda i, j: (0, j)),
        ]
        args += [w, b2]
    return pl.pallas_call(
        functools.partial(_inproj_kernel, glu=glu),
        out_shape=jax.ShapeDtypeStruct((n, nout), F32),
        grid=(n // tm, nj),
        in_specs=in_specs,
        out_specs=pl.BlockSpec((tm, tn), lambda i, j: (i, j)),
        scratch_shapes=[pltpu.VMEM((tm, d), BF16)],
        compiler_params=_cparams(2),
        name="inproj_glu" if glu else "inproj",
    )(*args)


CONV_ROWS = 64
CONV_CC = 256
CONV_GAP = 16


def _dwconv_kernel(u_ref, w_ref, bdw_ref, g_ref, b_ref, o_ref, pad_ref, v_ref, *, seg):
    rows, c = u_ref.shape
    width = w_ref.shape[0]
    half = width // 2
    nseg = rows // seg
    stride = seg + CONV_GAP
    zeros_gap = jnp.zeros((CONV_GAP, c), F32)
    for s in range(nseg):
        pad_ref[s * stride:s * stride + CONV_GAP, :] = zeros_gap
        pad_ref[s * stride + CONV_GAP:(s + 1) * stride, :] = u_ref[s * seg:(s + 1) * seg, :]
    pad_ref[nseg * stride:nseg * stride + CONV_GAP, :] = zeros_gap

    for rb in range(rows // CONV_ROWS):
        r0 = rb * CONV_ROWS
        s, within = divmod(r0, seg)
        base = s * stride + CONV_GAP + within - half

        def chunk(cc, carry, base=base, r0=r0):
            c0 = pl.multiple_of(cc * CONV_CC, CONV_CC)
            acc = jnp.zeros((CONV_ROWS, CONV_CC), F32) + bdw_ref[:, pl.ds(c0, CONV_CC)]
            for k in range(width):
                acc = acc + w_ref[k:k + 1, pl.ds(c0, CONV_CC)] * pad_ref[base + k:base + k + CONV_ROWS, pl.ds(c0, CONV_CC)]
            v_ref[r0:r0 + CONV_ROWS, pl.ds(c0, CONV_CC)] = acc
            return carry

        lax.fori_loop(0, c // CONV_CC, chunk, 0)

    ln_rows = 32

    def ln_block(r, carry):
        q0 = pl.multiple_of(r * ln_rows, ln_rows)
        v = v_ref[pl.ds(q0, ln_rows), :]
        mu = jnp.mean(v, axis=-1, keepdims=True)
        vc = v - mu
        var = jnp.mean(vc * vc, axis=-1, keepdims=True)
        y = vc * lax.rsqrt(var + LN_EPS) * g_ref[...] + b_ref[...]
        o_ref[pl.ds(q0, ln_rows), :] = (y * _sigmoid(y)).astype(o_ref.dtype)
        return carry

    lax.fori_loop(0, rows // ln_rows, ln_block, 0)


def _dwconv(u, row0, nrows, seg, w_dw, b_dw, g_n, b_n):
    c = u.shape[1]
    width = w_dw.shape[0]
    tile = ROW_TILE
    assert tile % seg == 0 and seg % CONV_ROWS == 0 and CONV_GAP >= width // 2
    off = row0 // tile
    nseg = tile // seg
    vec = lambda a: a.reshape(1, c)
    return pl.pallas_call(
        functools.partial(_dwconv_kernel, seg=seg),
        out_shape=jax.ShapeDtypeStruct((nrows, c), BF16),
        grid=(nrows // tile,),
        in_specs=[
            pl.BlockSpec((tile, c), lambda i: (i + off, 0)),
            pl.BlockSpec((width, c), lambda i: (0, 0)),
            pl.BlockSpec((1, c), lambda i: (0, 0)),
            pl.BlockSpec((1, c), lambda i: (0, 0)),
            pl.BlockSpec((1, c), lambda i: (0, 0)),
        ],
        out_specs=pl.BlockSpec((tile, c), lambda i: (i, 0)),
        scratch_shapes=[
            pltpu.VMEM((nseg * (seg + CONV_GAP) + CONV_GAP, c), F32),
            pltpu.VMEM((tile, c), F32),
        ],
        compiler_params=_cparams(1),
        name="dwconv_seg%d" % seg,
    )(u, w_dw, vec(b_dw), vec(g_n), vec(b_n))


def _outproj_kernel(a_ref, w_ref, bias_ref, x_ref, gate_ref, g_ref, b_ref, o_ref, acc_ref, *, alpha):
    k = pl.program_id(1)

    @pl.when(k == 0)
    def _():
        acc_ref[...] = jnp.zeros_like(acc_ref)

    acc_ref[...] += jnp.dot(a_ref[...], w_ref[...], preferred_element_type=F32)

    @pl.when(k == pl.num_programs(1) - 1)
    def _():
        m = acc_ref[...] + bias_ref[...]
        z = alpha * x_ref[...] + gate_ref[...] * m
        mu = jnp.mean(z, axis=-1, keepdims=True)
        zc = z - mu
        var = jnp.mean(zc * zc, axis=-1, keepdims=True)
        o_ref[...] = zc * lax.rsqrt(var + LN_EPS) * g_ref[...] + b_ref[...]


def _outproj(a, w, bias, x, mods, k_gate, g, b, mod_row, alpha, *, tm=256, tk=512):
    n, kdim = a.shape
    d = w.shape[1]
    vec = lambda v: v.reshape(1, d)
    row = lambda i: mod_row(i * (tm // ROW_TILE))
    return pl.pallas_call(
        functools.partial(_outproj_kernel, alpha=alpha),
        out_shape=jax.ShapeDtypeStruct((n, d), F32),
        grid=(n // tm, kdim // tk),
        in_specs=[
            pl.BlockSpec((tm, tk), lambda i, k: (i, k)),
            pl.BlockSpec((tk, d), lambda i, k: (k, 0)),
            pl.BlockSpec((1, d), lambda i, k: (0, 0)),
            pl.BlockSpec((tm, d), lambda i, k: (i, 0)),
            pl.BlockSpec((None, None, 1, d), lambda i, k: (k_gate, row(i), 0, 0)),
            pl.BlockSpec((1, d), lambda i, k: (0, 0)),
            pl.BlockSpec((1, d), lambda i, k: (0, 0)),
        ],
        out_specs=pl.BlockSpec((tm, d), lambda i, k: (i, 0)),
        scratch_shapes=[pltpu.VMEM((tm, d), F32)],
        compiler_params=_cparams(2),
        name="outproj_ln",
    )(a, w, vec(bias), x, mods, vec(g), vec(b))


def _lru_kernel(y_ref, xb_ref, wa_ref, wx_ref, p_ref, h0_ref, mixed_ref, fin_ref,
                a_ref, b_ref, hf_ref, hb_ref):
    t, hp = xb_ref.shape
    xb = xb_ref[...]
    p = p_ref[...]
    row = lax.broadcasted_iota(jnp.int32, (t, hp), 0)

    def from_above(x, s):
        return jnp.where(row >= s, pltpu.roll(x, s, 0), 0.0)

    def from_below(x, s):
        return jnp.where(row < t - s, pltpu.roll(x, t - s, 0), 0.0)

    xc = (p[0:1] * from_above(xb, 2) + p[1:2] * from_above(xb, 1) + p[2:3] * xb
          + p[3:4] * from_below(xb, 1) + p[4:5])
    xcb = xc.astype(BF16)
    for d in range(2):
        r = _sigmoid(jnp.dot(xcb, wa_ref[d], preferred_element_type=F32) + p[5 + d:6 + d])
        i = _sigmoid(jnp.dot(xcb, wx_ref[d], preferred_element_type=F32) + p[7 + d:8 + d])
        z = -p[9 + d:10 + d]
        softplus = jnp.maximum(z, 0.0) + jnp.log(1.0 + jnp.exp(-jnp.abs(z)))
        a = jnp.exp((-LRU_C) * softplus * r)
        a_ref[d] = a
        b_ref[d] = jnp.sqrt(1.0 - a * a) * (i * xc)

    grow = lax.broadcasted_iota(jnp.int32, (SUBLANE, hp), 0)
    ngroups = t // SUBLANE

    def group(g, carry):
        cf, cb = carry
        f0 = pl.multiple_of(g * SUBLANE, SUBLANE)
        af = a_ref[0, pl.ds(f0, SUBLANE), :]
        bf = b_ref[0, pl.ds(f0, SUBLANE), :]
        r0 = pl.multiple_of((ngroups - 1 - g) * SUBLANE, SUBLANE)
        ab = a_ref[1, pl.ds(r0, SUBLANE), :]
        bb = b_ref[1, pl.ds(r0, SUBLANE), :]
        for s in (1, 2, 4):
            mf = grow >= s
            bf = jnp.where(mf, af * pltpu.roll(bf, s, 0) + bf, bf)
            af = jnp.where(mf, af * pltpu.roll(af, s, 0), af)
            mb = grow < SUBLANE - s
            bb = jnp.where(mb, ab * pltpu.roll(bb, SUBLANE - s, 0) + bb, bb)
            ab = jnp.where(mb, ab * pltpu.roll(ab, SUBLANE - s, 0), ab)
        hf = af * cf + bf
        hb = ab * cb + bb
        hf_ref[pl.ds(f0, SUBLANE), :] = hf
        hb_ref[pl.ds(r0, SUBLANE), :] = hb
        cf = jnp.broadcast_to(hf[SUBLANE - 1:SUBLANE, :], (SUBLANE, hp))
        cb = jnp.broadcast_to(hb[0:1, :], (SUBLANE, hp))
        return cf, cb

    h0 = h0_ref[...]
    cf0 = jnp.broadcast_to(h0[0:1, :], (SUBLANE, hp))
    cb0 = jnp.broadcast_to(h0[1:2, :], (SUBLANE, hp))
    cf, cb = lax.fori_loop(0, ngroups, group, (cf0, cb0))
    fin_ref[0:1, :] = cf[0:1, :]
    fin_ref[1:2, :] = cb[0:1, :]

    y = y_ref[...]
    gelu = 0.5 * y * (1.0 + jnp.tanh(math.sqrt(2.0 / math.pi) * (y + 0.044715 * (y * y * y))))
    mixed_ref[...] = ((hf_ref[...] + hb_ref[...]) * gelu).astype(mixed_ref.dtype)


def _lru_core(yx, row0, n_seq, t, wa, wx, params, h0):
    nh, _, hp = params.shape
    dp = nh * hp
    off = row0 // t
    return pl.pallas_call(
        _lru_kernel,
        out_shape=(jax.ShapeDtypeStruct((n_seq * t, dp), BF16),
                   jax.ShapeDtypeStruct((n_seq, 2, dp), F32)),
        grid=(nh, n_seq),
        in_specs=[
            pl.BlockSpec((t, hp), lambda h, s: (s + off, h)),
            pl.BlockSpec((t, hp), lambda h, s: (s + off, h + nh)),
            pl.BlockSpec((2, None, hp, hp), lambda h, s: (0, h, 0, 0)),
            pl.BlockSpec((2, None, hp, hp), lambda h, s: (0, h, 0, 0)),
            pl.BlockSpec((None, 16, hp), lambda h, s: (h, 0, 0)),
            pl.BlockSpec((None, 2, hp), lambda h, s: (s, 0, h)),
        ],
        out_specs=(pl.BlockSpec((t, hp), lambda h, s: (s, h)),
                   pl.BlockSpec((None, 2, hp), lambda h, s: (s, 0, h))),
        scratch_shapes=[
            pltpu.VMEM((2, t, hp), F32), pltpu.VMEM((2, t, hp), F32),
            pltpu.VMEM((t, hp), F32), pltpu.VMEM((t, hp), F32),
        ],
        compiler_params=_cparams(2),
        name="lru_core_t%d" % t,
    )(yx, yx, wa, wx, params, h0)


def _router_kernel(x_ref, sh_ref, sc_ref, wh_ref, wl_ref, br_ref, xm_ref, ri_ref, rw_ref, *, n_groups, n_experts):
    xm = x_ref[...] * (1.0 + sc_ref[...]) + sh_ref[...]
    xm_ref[...] = xm
    xh = xm.astype(BF16)
    xl = (xm - xh.astype(F32)).astype(BF16)
    wh = wh_ref[...]
    logits = (jnp.dot(xh, wh, preferred_element_type=F32)
              + jnp.dot(xh, wl_ref[...], preferred_element_type=F32)
              + jnp.dot(xl, wh, preferred_element_type=F32)) + br_ref[...]
    tt = logits.shape[0]
    lane = lax.broadcasted_iota(jnp.int32, (tt, LANE), 1)
    neg = jnp.float32(-jnp.inf)
    big = jnp.int32(LANE)
    epg = n_experts // n_groups
    is_grp = (lane >= n_experts) & (lane < n_experts + n_groups)
    gl = jnp.where(is_grp, logits, neg)
    gmax = jnp.max(gl, axis=-1, keepdims=True)
    gsum = jnp.sum(jnp.where(is_grp, jnp.exp(gl - gmax), 0.0), axis=-1, keepdims=True)
    p_grp = 1.0 / gsum
    grp = jnp.min(jnp.where(gl == gmax, lane, big), axis=-1, keepdims=True) - n_experts
    in_grp = (lane >= grp * epg) & (lane < (grp + 1) * epg)
    el = jnp.where(in_grp, logits, neg)
    t1 = jnp.max(el, axis=-1, keepdims=True)
    i1 = jnp.min(jnp.where(el == t1, lane, big), axis=-1, keepdims=True)
    el2 = jnp.where(lane == i1, neg, el)
    t2 = jnp.max(el2, axis=-1, keepdims=True)
    i2 = jnp.min(jnp.where(el2 == t2, lane, big), axis=-1, keepdims=True)
    e2 = jnp.exp(t2 - t1)
    w1 = p_grp / (1.0 + e2)
    w2 = p_grp * e2 / (1.0 + e2)
    ri_ref[...] = jnp.where(lane == 0, i1, jnp.where(lane == 1, i2, 0))
    rw_ref[...] = jnp.where(lane == 0, w1, jnp.where(lane == 1, w2, 0.0))


def _router(x, mods, k_shift, k_scale, wr_hi, wr_lo, br, mod_row, n_groups, n_experts):
    n, d = x.shape
    tt = ROW_TILE
    return pl.pallas_call(
        functools.partial(_router_kernel, n_groups=n_groups, n_experts=n_experts),
        out_shape=(jax.ShapeDtypeStruct((n, d), F32),
                   jax.ShapeDtypeStruct((n, LANE), jnp.int32),
                   jax.ShapeDtypeStruct((n, LANE), F32)),
        grid=(n // tt,),
        in_specs=[
            pl.BlockSpec((tt, d), lambda i: (i, 0)),
            pl.BlockSpec((None, None, 1, d), lambda i: (k_shift, mod_row(i), 0, 0)),
            pl.BlockSpec((None, None, 1, d), lambda i: (k_scale, mod_row(i), 0, 0)),
            pl.BlockSpec((d, LANE), lambda i: (0, 0)),
            pl.BlockSpec((d, LANE), lambda i: (0, 0)),
            pl.BlockSpec((1, LANE), lambda i: (0, 0)),
        ],
        out_specs=(pl.BlockSpec((tt, d), lambda i: (i, 0)),
                   pl.BlockSpec((tt, LANE), lambda i: (i, 0)),
                   pl.BlockSpec((tt, LANE), lambda i: (i, 0))),
        compiler_params=_cparams(1),
        name="moe_router",
    )(x, mods, mods, wr_hi, wr_lo, br)


def _gather_kernel(idx_ref, x_hbm, o_hbm, sem):
    b = pl.program_id(0)
    tb = idx_ref.shape[-1]

    def issue(r, carry):
        pltpu.make_async_copy(x_hbm.at[pl.ds(idx_ref[0, 0, r], 1)],
                              o_hbm.at[pl.ds(b * tb + r, 1)], sem).start()
        return carry

    lax.fori_loop(0, tb, issue, 0)

    def drain(r, carry):
        pltpu.make_async_copy(x_hbm.at[pl.ds(0, 1)], o_hbm.at[pl.ds(b * tb + r, 1)], sem).wait()
        return carry

    lax.fori_loop(0, tb, drain, 0)


def _gather_rows(x, idx):
    nb, _, tb = idx.shape
    d = x.shape[1]
    return pl.pallas_call(
        _gather_kernel,
        out_shape=jax.ShapeDtypeStruct((nb * tb, d), x.dtype),
        grid=(nb,),
        in_specs=[
            pl.BlockSpec((1, 1, tb), lambda b: (b, 0, 0), memory_space=pltpu.SMEM),
            pl.BlockSpec(memory_space=pl.ANY),
        ],
        out_specs=pl.BlockSpec(memory_space=pl.ANY),
        scratch_shapes=[pltpu.SemaphoreType.DMA],
        compiler_params=_cparams(1),
        name="moe_gather",
    )(idx, x)


def _moe_up_kernel(se_ref, sc_ref, sb_ref, ob_ref, oc_ref, snew_ref, nvalid_ref,
                   xs_ref, w1_ref, w3_ref, h_ref, w1c_ref, w3c_ref):
    s = pl.program_id(0)

    @pl.when(snew_ref[s] == 1)
    def _():
        w1c_ref[...] = w1_ref[...].astype(BF16)
        w3c_ref[...] = w3_ref[...].astype(BF16)

    @pl.when(s < nvalid_ref[0])
    def _():
        xb = xs_ref[...].astype(BF16)
        h1 = jnp.dot(xb, w1c_ref[...], preferred_element_type=F32)
        h3 = jnp.dot(xb, w3c_ref[...], preferred_element_type=F32)
        h_ref[...] = (h1 * _sigmoid(h1) * h3).astype(h_ref.dtype)

    @pl.when(s >= nvalid_ref[0])
    def _():
        h_ref[...] = jnp.zeros_like(h_ref)


def _moe_up(tables, xs, w1, w3, fc):
    slots, d = xs.shape
    f = w1.shape[2]
    nsteps = tables[0].shape[0]
    return pl.pallas_call(
        _moe_up_kernel,
        out_shape=jax.ShapeDtypeStruct((slots, f), BF16),
        grid_spec=pltpu.PrefetchScalarGridSpec(
            num_scalar_prefetch=7,
            grid=(nsteps,),
            in_specs=[
                pl.BlockSpec((MOE_TB, d), lambda s, se, sc, sb, ob, oc, sn, nv: (sb[s], 0)),
                pl.BlockSpec((None, d, fc), lambda s, se, sc, sb, ob, oc, sn, nv: (se[s], 0, sc[s])),
                pl.BlockSpec((None, d, fc), lambda s, se, sc, sb, ob, oc, sn, nv: (se[s], 0, sc[s])),
            ],
            out_specs=pl.BlockSpec((MOE_TB, fc), lambda s, se, sc, sb, ob, oc, sn, nv: (ob[s], oc[s])),
            scratch_shapes=[pltpu.VMEM((d, fc), BF16), pltpu.VMEM((d, fc), BF16)],
        ),
        compiler_params=_cparams(1),
        name="moe_up",
    )(*tables, xs, w1, w3)


def _moe_down_kernel(se_ref, sc_ref, sb_ref, ob_ref, oc_ref, snew_ref, nvalid_ref,
                     h_ref, w2_ref, y_ref, w2c_ref):
    s = pl.program_id(0)

    @pl.when(snew_ref[s] == 1)
    def _():
        w2c_ref[...] = w2_ref[...].astype(BF16)

    @pl.when(s < nvalid_ref[0])
    def _():
        y_ref[...] = jnp.dot(h_ref[...], w2c_ref[...], preferred_element_type=F32)

    @pl.when(s >= nvalid_ref[0])
    def _():
        y_ref[...] = jnp.zeros_like(y_ref)


def _moe_down(tables, h, w2, nc):
    slots, f = h.shape
    d = w2.shape[2]
    nsteps = tables[0].shape[0]
    return pl.pallas_call(
        _moe_down_kernel,
        out_shape=jax.ShapeDtypeStruct((slots, d), F32),
        grid_spec=pltpu.PrefetchScalarGridSpec(
            num_scalar_prefetch=7,
            grid=(nsteps,),
            in_specs=[
                pl.BlockSpec((MOE_TB, f), lambda s, se, sc, sb, ob, oc, sn, nv: (sb[s], 0)),
                pl.BlockSpec((None, f, nc), lambda s, se, sc, sb, ob, oc, sn, nv: (se[s], 0, sc[s])),
            ],
            out_specs=pl.BlockSpec((MOE_TB, nc), lambda s, se, sc, sb, ob, oc, sn, nv: (ob[s], oc[s])),
            scratch_shapes=[pltpu.VMEM((f, nc), BF16)],
        ),
        compiler_params=_cparams(1),
        name="moe_down",
    )(*tables, h, w2)


def _combine_kernel(dest_ref, y_hbm, x_ref, rw_ref, gate_ref, g_ref, b_ref, o_ref, ybuf_ref, sem, *, alpha):
    tt = x_ref.shape[0]

    def issue(r, carry):
        for k in range(TOP_K):
            pltpu.make_async_copy(y_hbm.at[pl.ds(dest_ref[0, 0, k * tt + r], 1)],
                                  ybuf_ref.at[k, pl.ds(r, 1)], sem).start()
        return carry

    lax.fori_loop(0, tt, issue, 0)

    def drain(r, carry):
        for k in range(TOP_K):
            pltpu.make_async_copy(y_hbm.at[pl.ds(0, 1)], ybuf_ref.at[k, pl.ds(r, 1)], sem).wait()
        return carry

    lax.fori_loop(0, tt, drain, 0)

    rw = rw_ref[...]
    f = rw[:, 0:1] * ybuf_ref[0] + rw[:, 1:2] * ybuf_ref[1]
    z = alpha * x_ref[...] + gate_ref[...] * f
    mu = jnp.mean(z, axis=-1, keepdims=True)
    zc = z - mu
    var = jnp.mean(zc * zc, axis=-1, keepdims=True)
    o_ref[...] = zc * lax.rsqrt(var + LN_EPS) * g_ref[...] + b_ref[...]


def _combine(dest, y, x, rw, mods, k_gate, g, b, mod_row, alpha):
    n, d = x.shape
    tt = ROW_TILE
    vec = lambda v: v.reshape(1, d)
    return pl.pallas_call(
        functools.partial(_combine_kernel, alpha=alpha),
        out_shape=jax.ShapeDtypeStruct((n, d), F32),
        grid=(n // tt,),
        in_specs=[
            pl.BlockSpec((1, 1, TOP_K * tt), lambda i: (i, 0, 0), memory_space=pltpu.SMEM),
            pl.BlockSpec(memory_space=pl.ANY),
            pl.BlockSpec((tt, d), lambda i: (i, 0)),
            pl.BlockSpec((tt, LANE), lambda i: (i, 0)),
            pl.BlockSpec((None, None, 1, d), lambda i: (k_gate, mod_row(i), 0, 0)),
            pl.BlockSpec((1, d), lambda i: (0, 0)),
            pl.BlockSpec((1, d), lambda i: (0, 0)),
        ],
        out_specs=pl.BlockSpec((tt, d), lambda i: (i, 0)),
        scratch_shapes=[pltpu.VMEM((TOP_K, tt, d), F32), pltpu.SemaphoreType.DMA],
        compiler_params=_cparams(1),
        name="moe_combine_ln",
    )(dest, y, x, rw, mods, vec(g), vec(b))


def _dispatch_tables(flat_e, n_experts, n_chunks_up, n_chunks_down):
    nk = flat_e.shape[0]
    tb = MOE_TB
    nb = nk // tb + n_experts
    onehot = (flat_e[:, None] == jnp.arange(n_experts, dtype=jnp.int32)[None, :]).astype(jnp.int32)
    csum = jnp.cumsum(onehot, axis=0)
    counts = csum[-1]
    rank = jnp.take_along_axis(csum, flat_e[:, None], axis=1)[:, 0] - 1
    blocks_e = (counts + tb - 1) // tb
    blk_end = jnp.cumsum(blocks_e)
    blk_start = blk_end - blocks_e
    dest = blk_start[flat_e] * tb + rank
    slot_tok = jnp.zeros((nb * tb,), jnp.int32).at[dest].set(jnp.arange(nk, dtype=jnp.int32) // TOP_K)
    n_blocks = blk_end[-1]

    def steps(n_chunks):
        ns = nb * n_chunks
        per_e = blocks_e * n_chunks
        e_end = jnp.cumsum(per_e)
        e_start = e_end - per_e
        s = jnp.arange(ns, dtype=jnp.int32)
        nvalid = e_end[-1]
        sv = jnp.minimum(s, nvalid - 1)
        e = jnp.minimum(jnp.searchsorted(e_end, sv, side='right'), n_experts - 1).astype(jnp.int32)
        local = sv - e_start[e]
        be = jnp.maximum(blocks_e[e], 1)
        chunk = (local // be).astype(jnp.int32)
        blk = (blk_start[e] + local % be).astype(jnp.int32)
        key = e * n_chunks + chunk
        new = jnp.concatenate([jnp.ones((1,), jnp.int32), (key[1:] != key[:-1]).astype(jnp.int32)])
        extra = s - nvalid
        out_blk = jnp.where(s < nvalid, blk, n_blocks + extra // n_chunks).astype(jnp.int32)
        out_chunk = jnp.where(s < nvalid, chunk, extra % n_chunks).astype(jnp.int32)
        return (e, chunk, blk, out_blk, out_chunk, new, nvalid.reshape(1).astype(jnp.int32))

    return dest, slot_tok.reshape(nb, 1, tb), steps(n_chunks_up), steps(n_chunks_down), n_blocks


def _hier_moe_block(x, mods, ln_g, ln_b, rg_w, rg_b, re_w, re_b, w1, w3, w2, mod_row, alpha):
    n, d = x.shape
    n_groups = rg_w.shape[1]
    n_experts = re_w.shape[1]
    f = w1.shape[2]
    wr = jnp.zeros((d, LANE), F32).at[:, :n_experts].set(re_w).at[:, n_experts:n_experts + n_groups].set(rg_w)
    br = jnp.zeros((1, LANE), F32).at[0, :n_experts].set(re_b).at[0, n_experts:n_experts + n_groups].set(rg_b)
    wr_hi = wr.astype(BF16)
    wr_lo = (wr - wr_hi.astype(F32)).astype(BF16)
    xm, ri, rw = _router(x, mods, 3, 4, wr_hi, wr_lo, br, mod_row, n_groups, n_experts)
    flat_e = ri[:, :TOP_K].reshape(-1)
    fc = min(512, f)
    nc = min(2048, d)
    dest, slot_tok, up_tables, down_tables, _ = _dispatch_tables(flat_e, n_experts, f // fc, d // nc)
    xs = _gather_rows(xm, slot_tok)
    h = _moe_up(up_tables, xs, w1, w3, fc)
    y = _moe_down(down_tables, h, w2, nc)
    tt = ROW_TILE
    dest_t = dest.reshape(n // tt, tt, TOP_K).transpose(0, 2, 1).reshape(n // tt, 1, TOP_K * tt)
    return _combine(dest_t, y, x, rw, mods, 5, ln_g, ln_b, mod_row, alpha)


def _pad_heads(v, nh, hp):
    lead = v.shape[:-1]
    hb = v.shape[-1] // nh
    v = v.reshape(lead + (nh, hb))
    v = jnp.pad(v, [(0, 0)] * len(lead) + [(0, 0), (0, hp - hb)])
    return v.reshape(lead + (nh * hp,))


def kernel(x_prompt, x_sample, state_lru, c, c_ctx, mod_w, mod_b, ln_g, ln_b, conv_w_in, conv_b_in, conv_w_dw, conv_b_dw, conv_ln_g, conv_ln_b, conv_w_out, conv_b_out, lru_w_in, lru_b_in, lru_w_sc, lru_b_sc, lru_w_a, lru_b_a, lru_w_x, lru_b_x, lru_lam, lru_w_out, lru_b_out, moe_rg_w, moe_rg_b, moe_re_w, moe_re_b, moe_w1, moe_w3, moe_w2):
    batch, seq, d = x_prompt.shape
    dec_batch, dec_seq, _ = x_sample.shape
    depth = mod_w.shape[0]
    n_p = batch * seq
    n_s = dec_batch * dec_seq
    alpha = (2 * depth) ** 0.25
    assert n_p % 512 == 0 and dec_seq % 512 == 0 and seq % ROW_TILE == 0 and ROW_TILE % GRID_W == 0

    x = jnp.concatenate([x_prompt.reshape(n_p, d), x_sample.reshape(n_s, d)], axis=0)

    n_rows = _round_up(dec_batch + 1, SUBLANE)
    cond = jnp.zeros((n_rows, d), F32).at[:dec_batch].set(c).at[dec_batch].set(c_ctx)
    m_all = _adaln(cond, mod_w, mod_b)
    m_all = m_all.reshape(depth, n_rows, 6, 1, d).transpose(0, 2, 1, 3, 4)
    mod_row = _mod_row_fn(ROW_TILE, n_p, dec_seq, dec_batch)

    new_states = []
    for l in range(depth):
        j = l // 2
        mods = m_all[l]
        if l % 2 == 0:
            u = _inproj(x, mods, 0, 1, conv_w_in[j].astype(BF16), conv_b_in[j], mod_row, glu=True)
            dw = (conv_w_dw[j], conv_b_dw[j], conv_ln_g[j], conv_ln_b[j])
            vp = _dwconv(u, 0, n_p, seq, *dw)
            vs = _dwconv(u, n_p, n_s, GRID_W, *dw)
            act = jnp.concatenate([vp, vs], axis=0)
            w_out = conv_w_out[j].astype(BF16)
            b_out = conv_b_out[j]
        else:
            nh = lru_w_a.shape[2]
            d_rnn = lru_w_sc.shape[2]
            hp = _round_up(d_rnn // nh, LANE)
            ph = functools.partial(_pad_heads, nh=nh, hp=hp)
            w_in = lru_w_in[j]
            w_in_p = jnp.concatenate([ph(w_in[:, :d_rnn]), ph(w_in[:, d_rnn:])], axis=1).astype(BF16)
            b_in = lru_b_in[j]
            b_in_p = jnp.concatenate([ph(b_in[:d_rnn]), ph(b_in[d_rnn:])])
            yx = _inproj(x, mods, 0, 1, w_in_p, b_in_p, mod_row, glu=False)

            def pad_sq(w):
                hb = w.shape[-1]
                return jnp.pad(w, [(0, 0), (0, 0), (0, hp - hb), (0, hp - hb)]).astype(BF16)

            rows = jnp.concatenate([
                ph(lru_w_sc[j]), ph(lru_b_sc[j])[None], ph(lru_b_a[j]), ph(lru_b_x[j]), ph(lru_lam[j])], axis=0)
            params = jnp.zeros((16, nh * hp), F32).at[:rows.shape[0]].set(rows)
            params = params.reshape(16, nh, hp).transpose(1, 0, 2)
            wa = pad_sq(lru_w_a[j])
            wx = pad_sq(lru_w_x[j])
            h0_p = jnp.zeros((batch, 2, nh * hp), F32)
            h0_s = ph(state_lru[:, j])
            mixed_p, fin_p = _lru_core(yx, 0, batch, seq, wa, wx, params, h0_p)
            mixed_s, _ = _lru_core(yx, n_p, dec_batch, dec_seq, wa, wx, params, h0_s)
            act = jnp.concatenate([mixed_p, mixed_s], axis=0)
            hb = d_rnn // nh
            new_states.append(fin_p.reshape(batch, 2, nh, hp)[..., :hb].reshape(batch, 2, d_rnn))
            w_out = jnp.pad(lru_w_out[j].reshape(nh, hb, d), [(0, 0), (0, hp - hb), (0, 0)]).reshape(nh * hp, d).astype(BF16)
            b_out = lru_b_out[j]
        x = _outproj(act, w_out, b_out, x, mods, 2, ln_g[l, 0], ln_b[l, 0], mod_row, alpha)
        x = _hier_moe_block(x, mods, ln_g[l, 1], ln_b[l, 1], moe_rg_w[l], moe_rg_b[l], moe_re_w[l], moe_re_b[l],
                            moe_w1[l], moe_w3[l], moe_w2[l], mod_row, alpha)

    y_prompt = x[:n_p].reshape(batch, seq, d)
    y_sample = x[n_p:].reshape(dec_batch, dec_seq, d)
    new_state_lru = jnp.stack(new_states, axis=1).astype(x_prompt.dtype)
    return (y_prompt, y_sample, new_state_lru)
```

```python
import functools
import math

import jax
import jax.numpy as jnp
from jax import lax
from jax.experimental import pallas as pl
from jax.experimental.pallas import tpu as pltpu

F32 = jnp.float32
BF16 = jnp.bfloat16

GRID_W = 64
TOP_K = 2
LRU_C = 8.0
LN_EPS = 1e-5
LANE = 128
SUBLANE = 8
VMEM_LIMIT = 56 * 1024 * 1024
MOE_TB = 256
ROW_TILE = 256


def _sigmoid(x):
    return 1.0 / (1.0 + jnp.exp(-x))


def _round_up(x, m):
    return (x + m - 1) // m * m


def _cparams(n_axes):
    return pltpu.CompilerParams(
        dimension_semantics=("arbitrary",) * n_axes, vmem_limit_bytes=VMEM_LIMIT)


def _mod_row_fn(tile, n_prompt, dec_seq, ctx_row):
    def fn(i):
        t0 = i * tile
        return jnp.where(t0 < n_prompt, ctx_row, (t0 - n_prompt) // dec_seq)
    return fn


def _adaln_kernel(c_ref, w_ref, b_ref, o_ref):
    a = c_ref[...]
    a = (a * _sigmoid(a)).astype(BF16)
    o_ref[...] = jnp.dot(a, w_ref[...].astype(BF16), preferred_element_type=F32) + b_ref[...]


def _adaln(cond, mod_w, mod_b):
    depth, d, n6 = mod_w.shape
    r = cond.shape[0]
    tn = 512
    return pl.pallas_call(
        _adaln_kernel,
        out_shape=jax.ShapeDtypeStruct((depth, r, n6), F32),
        grid=(depth, n6 // tn),
        in_specs=[
            pl.BlockSpec((r, d), lambda l, j: (0, 0)),
            pl.BlockSpec((None, d, tn), lambda l, j: (l, 0, j)),
            pl.BlockSpec((None, 1, tn), lambda l, j: (l, 0, j)),
        ],
        out_specs=pl.BlockSpec((None, r, tn), lambda l, j: (l, 0, j)),
        compiler_params=_cparams(2),
        name="adaln",
    )(cond, mod_w, mod_b.reshape(depth, 1, n6))


def _inproj_kernel(x_ref, sh_ref, sc_ref, *rest, glu):
    if glu:
        wa_ref, wg_ref, ba_ref, bg_ref, o_ref, xs_ref = rest
    else:
        wa_ref, ba_ref, o_ref, xs_ref = rest

    @pl.when(pl.program_id(1) == 0)
    def _():
        xs_ref[...] = (x_ref[...] * (1.0 + sc_ref[...]) + sh_ref[...]).astype(BF16)

    xs = xs_ref[...]
    a = jnp.dot(xs, wa_ref[...], preferred_element_type=F32) + ba_ref[...]
    if glu:
        g = jnp.dot(xs, wg_ref[...], preferred_element_type=F32) + bg_ref[...]
        a = a * _sigmoid(g)
    o_ref[...] = a.astype(o_ref.dtype)


def _inproj(x, mods, k_shift, k_scale, w, b, mod_row, *, glu, tm=512, tn=512):
    n, d = x.shape
    nw = w.shape[1]
    nout = nw // 2 if glu else nw
    nj = nout // tn
    b2 = b.reshape(1, nw)
    row = lambda i, j: mod_row(i * (tm // ROW_TILE))
    in_specs = [
        pl.BlockSpec((tm, d), lambda i, j: (i, 0)),
        pl.BlockSpec((None, None, 1, d), lambda i, j: (k_shift, row(i, j), 0, 0)),
        pl.BlockSpec((None, None, 1, d), lambda i, j: (k_scale, row(i, j), 0, 0)),
    ]
    args = [x, mods, mods]
    if glu:
        in_specs += [
            pl.BlockSpec((d, tn), lambda i, j: (0, j)),
            pl.BlockSpec((d, tn), lambda i, j: (0, j + nj)),
            pl.BlockSpec((1, tn), lambda i, j: (0, j)),
            pl.BlockSpec((1, tn), lambda i, j: (0, j + nj)),
        ]
        args += [w, w, b2, b2]
    else:
        in_specs += [
            pl.BlockSpec((d, tn), lambda i, j: (0, j)),
            pl.BlockSpec((1, tn), lambda i, j: (0, j)),
        ]
        args += [w, b2]
    return pl.pallas_call(
        functools.partial(_inproj_kernel, glu=glu),
        out_shape=jax.ShapeDtypeStruct((n, nout), F32),
        grid=(n // tm, nj),
        in_specs=in_specs,
        out_specs=pl.BlockSpec((tm, tn), lambda i, j: (i, j)),
        scratch_shapes=[pltpu.VMEM((tm, d), BF16)],
        compiler_params=_cparams(2),
        name="inproj_glu" if glu else "inproj",
    )(*args)


CONV_ROWS = 64
CONV_CC = 256
CONV_GAP = 16
CONV_CHUNK = 1024


def _dwconv_kernel(u_ref, w_ref, bdw_ref, g_ref, b_ref, o_ref, sh_ref, v_ref, *, seg):
    rows, c = u_ref.shape
    width = w_ref.shape[0]
    half = width // 2
    nseg = rows // seg
    stride = seg + CONV_GAP
    pad_rows = nseg * stride + CONV_GAP
    chunk_w = sh_ref.shape[2]
    zeros_gap = jnp.zeros((CONV_GAP, chunk_w), F32)

    def lane_chunk(ci, carry):
        l0 = pl.multiple_of(ci * chunk_w, chunk_w)
        for s in range(nseg):
            sh_ref[0, s * stride:s * stride + CONV_GAP, :] = zeros_gap
            sh_ref[0, s * stride + CONV_GAP:(s + 1) * stride, :] = u_ref[s * seg:(s + 1) * seg, pl.ds(l0, chunk_w)]
        sh_ref[0, nseg * stride:pad_rows, :] = zeros_gap
        for j in range(1, SUBLANE):
            sh_ref[j, 0:pad_rows - SUBLANE, :] = sh_ref[0, j:j + pad_rows - SUBLANE, :]

        for rb in range(rows // CONV_ROWS):
            r0 = rb * CONV_ROWS
            s, within = divmod(r0, seg)
            base = s * stride + CONV_GAP + within - half

            def sub(cc, carry2, base=base, r0=r0):
                c0 = pl.multiple_of(cc * CONV_CC, CONV_CC)
                g0 = pl.multiple_of(l0 + c0, CONV_CC)
                acc = jnp.zeros((CONV_ROWS, CONV_CC), F32) + bdw_ref[:, pl.ds(g0, CONV_CC)]
                for k in range(width):
                    j = (base + k) % SUBLANE
                    q = base + k - j
                    acc = acc + w_ref[k:k + 1, pl.ds(g0, CONV_CC)] * sh_ref[j, q:q + CONV_ROWS, pl.ds(c0, CONV_CC)]
                v_ref[r0:r0 + CONV_ROWS, pl.ds(g0, CONV_CC)] = acc
                return carry2

            lax.fori_loop(0, chunk_w // CONV_CC, sub, 0)
        return carry

    lax.fori_loop(0, c // chunk_w, lane_chunk, 0)

    ln_rows = 32

    def ln_block(r, carry):
        q0 = pl.multiple_of(r * ln_rows, ln_rows)
        v = v_ref[pl.ds(q0, ln_rows), :]
        mu = jnp.mean(v, axis=-1, keepdims=True)
        vc = v - mu
        var = jnp.mean(vc * vc, axis=-1, keepdims=True)
        y = vc * lax.rsqrt(var + LN_EPS) * g_ref[...] + b_ref[...]
        o_ref[pl.ds(q0, ln_rows), :] = (y * _sigmoid(y)).astype(o_ref.dtype)
        return carry

    lax.fori_loop(0, rows // ln_rows, ln_block, 0)


def _dwconv(u, row0, nrows, seg, w_dw, b_dw, g_n, b_n):
    c = u.shape[1]
    width = w_dw.shape[0]
    tile = ROW_TILE
    assert tile % seg == 0 and seg % CONV_ROWS == 0 and CONV_GAP >= width // 2
    off = row0 // tile
    nseg = tile // seg
    vec = lambda a: a.reshape(1, c)
    return pl.pallas_call(
        functools.partial(_dwconv_kernel, seg=seg),
        out_shape=jax.ShapeDtypeStruct((nrows, c), BF16),
        grid=(nrows // tile,),
        in_specs=[
            pl.BlockSpec((tile, c), lambda i: (i + off, 0)),
            pl.BlockSpec((width, c), lambda i: (0, 0)),
            pl.BlockSpec((1, c), lambda i: (0, 0)),
            pl.BlockSpec((1, c), lambda i: (0, 0)),
            pl.BlockSpec((1, c), lambda i: (0, 0)),
        ],
        out_specs=pl.BlockSpec((tile, c), lambda i: (i, 0)),
        scratch_shapes=[
            pltpu.VMEM((SUBLANE, nseg * (seg + CONV_GAP) + CONV_GAP, min(CONV_CHUNK, c)), F32),
            pltpu.VMEM((tile, c), F32),
        ],
        compiler_params=_cparams(1),
        name="dwconv_seg%d" % seg,
    )(u, w_dw, vec(b_dw), vec(g_n), vec(b_n))


def _outproj_kernel(a_ref, w_ref, bias_ref, x_ref, gate_ref, o_ref, *, alpha):
    m = jnp.dot(a_ref[...], w_ref[...], preferred_element_type=F32) + bias_ref[...]
    o_ref[...] = alpha * x_ref[...] + gate_ref[...] * m


def _outproj(a, w, bias, x, mods, k_gate, mod_row, alpha, *, tm=512, tn=512):
    n, kdim = a.shape
    d = w.shape[1]
    tn = min(tn, d)
    row = lambda i: mod_row(i * (tm // ROW_TILE))
    return pl.pallas_call(
        functools.partial(_outproj_kernel, alpha=alpha),
        out_shape=jax.ShapeDtypeStruct((n, d), F32),
        grid=(n // tm, d // tn),
        in_specs=[
            pl.BlockSpec((tm, kdim), lambda i, j: (i, 0)),
            pl.BlockSpec((kdim, tn), lambda i, j: (0, j)),
            pl.BlockSpec((1, tn), lambda i, j: (0, j)),
            pl.BlockSpec((tm, tn), lambda i, j: (i, j)),
            pl.BlockSpec((None, None, 1, tn), lambda i, j: (k_gate, row(i), 0, j)),
        ],
        out_specs=pl.BlockSpec((tm, tn), lambda i, j: (i, j)),
        compiler_params=_cparams(2),
        name="outproj",
    )(a, w, bias.reshape(1, d), x, mods)


def _lru_kernel(y_ref, xb_ref, wa_ref, wx_ref, p_ref, h0_ref, mixed_ref, fin_ref,
                a_ref, b_ref, hf_ref, hb_ref):
    t, hp = xb_ref.shape
    xb = xb_ref[...]
    p = p_ref[...]
    row = lax.broadcasted_iota(jnp.int32, (t, hp), 0)

    def from_above(x, s):
        return jnp.where(row >= s, pltpu.roll(x, s, 0), 0.0)

    def from_below(x, s):
        return jnp.where(row < t - s, pltpu.roll(x, t - s, 0), 0.0)

    xc = (p[0:1] * from_above(xb, 2) + p[1:2] * from_above(xb, 1) + p[2:3] * xb
          + p[3:4] * from_below(xb, 1) + p[4:5])
    xcb = xc.astype(BF16)
    for d in range(2):
        r = _sigmoid(jnp.dot(xcb, wa_ref[d], preferred_element_type=F32) + p[5 + d:6 + d])
        i = _sigmoid(jnp.dot(xcb, wx_ref[d], preferred_element_type=F32) + p[7 + d:8 + d])
        z = -p[9 + d:10 + d]
        softplus = jnp.maximum(z, 0.0) + jnp.log(1.0 + jnp.exp(-jnp.abs(z)))
        a = jnp.exp((-LRU_C) * softplus * r)
        a_ref[d] = a
        b_ref[d] = jnp.sqrt(1.0 - a * a) * (i * xc)

    grow = lax.broadcasted_iota(jnp.int32, (SUBLANE, hp), 0)
    ngroups = t // SUBLANE

    def group(g, carry):
        cf, cb = carry
        f0 = pl.multiple_of(g * SUBLANE, SUBLANE)
        af = a_ref[0, pl.ds(f0, SUBLANE), :]
        bf = b_ref[0, pl.ds(f0, SUBLANE), :]
        r0 = pl.multiple_of((ngroups - 1 - g) * SUBLANE, SUBLANE)
        ab = a_ref[1, pl.ds(r0, SUBLANE), :]
        bb = b_ref[1, pl.ds(r0, SUBLANE), :]
        for s in (1, 2, 4):
            mf = grow >= s
            bf = jnp.where(mf, af * pltpu.roll(bf, s, 0) + bf, bf)
            af = jnp.where(mf, af * pltpu.roll(af, s, 0), af)
            mb = grow < SUBLANE - s
            bb = jnp.where(mb, ab * pltpu.roll(bb, SUBLANE - s, 0) + bb, bb)
            ab = jnp.where(mb, ab * pltpu.roll(ab, SUBLANE - s, 0), ab)
        hf = af * cf + bf
        hb = ab * cb + bb
        hf_ref[pl.ds(f0, SUBLANE), :] = hf
        hb_ref[pl.ds(r0, SUBLANE), :] = hb
        cf = jnp.broadcast_to(hf[SUBLANE - 1:SUBLANE, :], (SUBLANE, hp))
        cb = jnp.broadcast_to(hb[0:1, :], (SUBLANE, hp))
        return cf, cb

    h0 = h0_ref[...]
    cf0 = jnp.broadcast_to(h0[0:1, :], (SUBLANE, hp))
    cb0 = jnp.broadcast_to(h0[1:2, :], (SUBLANE, hp))
    cf, cb = lax.fori_loop(0, ngroups, group, (cf0, cb0))
    fin_ref[0:1, :] = cf[0:1, :]
    fin_ref[1:2, :] = cb[0:1, :]

    y = y_ref[...]
    gelu = 0.5 * y * (1.0 + jnp.tanh(math.sqrt(2.0 / math.pi) * (y + 0.044715 * (y * y * y))))
    mixed_ref[...] = ((hf_ref[...] + hb_ref[...]) * gelu).astype(mixed_ref.dtype)


def _lru_core(yx, row0, n_seq, t, wa, wx, params, h0):
    nh, _, hp = params.shape
    dp = nh * hp
    off = row0 // t
    return pl.pallas_call(
        _lru_kernel,
        out_shape=(jax.ShapeDtypeStruct((n_seq * t, dp), BF16),
                   jax.ShapeDtypeStruct((n_seq, 2, dp), F32)),
        grid=(nh, n_seq),
        in_specs=[
            pl.BlockSpec((t, hp), lambda h, s: (s + off, h)),
            pl.BlockSpec((t, hp), lambda h, s: (s + off, h + nh)),
            pl.BlockSpec((2, None, hp, hp), lambda h, s: (0, h, 0, 0)),
            pl.BlockSpec((2, None, hp, hp), lambda h, s: (0, h, 0, 0)),
            pl.BlockSpec((None, 16, hp), lambda h, s: (h, 0, 0)),
            pl.BlockSpec((None, 2, hp), lambda h, s: (s, 0, h)),
        ],
        out_specs=(pl.BlockSpec((t, hp), lambda h, s: (s, h)),
                   pl.BlockSpec((None, 2, hp), lambda h, s: (s, 0, h))),
        scratch_shapes=[
            pltpu.VMEM((2, t, hp), F32), pltpu.VMEM((2, t, hp), F32),
            pltpu.VMEM((t, hp), F32), pltpu.VMEM((t, hp), F32),
        ],
        compiler_params=_cparams(2),
        name="lru_core_t%d" % t,
    )(yx, yx, wa, wx, params, h0)


def _router_kernel(z_ref, g_ref, b_ref, sh_ref, sc_ref, wh_ref, wl_ref, br_ref,
                   x_ref, xm_ref, ri_ref, rw_ref, cnt_ref, *, n_groups, n_experts):
    z = z_ref[...]
    mu = jnp.mean(z, axis=-1, keepdims=True)
    zc = z - mu
    var = jnp.mean(zc * zc, axis=-1, keepdims=True)
    x = zc * lax.rsqrt(var + LN_EPS) * g_ref[...] + b_ref[...]
    x_ref[...] = x
    xm = x * (1.0 + sc_ref[...]) + sh_ref[...]
    xm_ref[...] = xm
    xh = xm.astype(BF16)
    xl = (xm - xh.astype(F32)).astype(BF16)
    wh = wh_ref[...]
    logits = (jnp.dot(xh, wh, preferred_element_type=F32)
              + jnp.dot(xh, wl_ref[...], preferred_element_type=F32)
              + jnp.dot(xl, wh, preferred_element_type=F32)) + br_ref[...]
    tt = logits.shape[0]
    lane = lax.broadcasted_iota(jnp.int32, (tt, LANE), 1).astype(F32)
    neg = jnp.float32(-jnp.inf)
    big = jnp.float32(LANE)
    epg = n_experts // n_groups
    is_grp = (lane >= n_experts) & (lane < n_experts + n_groups)
    gl = jnp.where(is_grp, logits, neg)
    gmax = jnp.max(gl, axis=-1, keepdims=True)
    gsum = jnp.sum(jnp.where(is_grp, jnp.exp(gl - gmax), 0.0), axis=-1, keepdims=True)
    p_grp = 1.0 / gsum
    grp = jnp.min(jnp.where(gl == gmax, lane, big), axis=-1, keepdims=True) - n_experts
    in_grp = (lane >= grp * epg) & (lane < (grp + 1.0) * epg)
    el = jnp.where(in_grp, logits, neg)
    t1 = jnp.max(el, axis=-1, keepdims=True)
    i1 = jnp.min(jnp.where(el == t1, lane, big), axis=-1, keepdims=True)
    el2 = jnp.where(lane == i1, neg, el)
    t2 = jnp.max(el2, axis=-1, keepdims=True)
    i2 = jnp.min(jnp.where(el2 == t2, lane, big), axis=-1, keepdims=True)
    e2 = jnp.exp(t2 - t1)
    w1 = p_grp / (1.0 + e2)
    w2 = p_grp * e2 / (1.0 + e2)

    @pl.when(pl.program_id(0) == 0)
    def _():
        cnt_ref[...] = jnp.zeros_like(cnt_ref)

    sel1 = lane == i1
    sel2 = lane == i2
    onehot = jnp.where(sel1 | sel2, 1.0, 0.0)
    tr = lax.broadcasted_iota(jnp.int32, (tt, tt), 0)
    tc = lax.broadcasted_iota(jnp.int32, (tt, tt), 1)
    before = jnp.where(tc < tr, 1.0, 0.0).astype(BF16)
    prior = jnp.dot(before, onehot.astype(BF16), preferred_element_type=F32) + cnt_ref[...]
    r1 = jnp.sum(jnp.where(sel1, prior, 0.0), axis=-1, keepdims=True)
    r2 = jnp.sum(jnp.where(sel2, prior, 0.0), axis=-1, keepdims=True)
    cnt_ref[...] += jnp.sum(onehot, axis=0, keepdims=True)
    ri = jnp.where(lane == 0, i1, jnp.where(lane == 1, i2, jnp.where(lane == 2, r1, jnp.where(lane == 3, r2, 0.0))))
    ri_ref[...] = ri.astype(jnp.int32)
    rw_ref[...] = jnp.where(lane == 0, w1, jnp.where(lane == 1, w2, 0.0))


def _router(z, g, b, mods, k_shift, k_scale, wr_hi, wr_lo, br, mod_row, n_groups, n_experts):
    n, d = z.shape
    tt = ROW_TILE
    vec = lambda v: v.reshape(1, d)
    return pl.pallas_call(
        functools.partial(_router_kernel, n_groups=n_groups, n_experts=n_experts),
        out_shape=(jax.ShapeDtypeStruct((n, d), F32),
                   jax.ShapeDtypeStruct((n, d), F32),
                   jax.ShapeDtypeStruct((n, LANE), jnp.int32),
                   jax.ShapeDtypeStruct((n, LANE), F32),
                   jax.ShapeDtypeStruct((1, LANE), F32)),
        grid=(n // tt,),
        in_specs=[
            pl.BlockSpec((tt, d), lambda i: (i, 0)),
            pl.BlockSpec((1, d), lambda i: (0, 0)),
            pl.BlockSpec((1, d), lambda i: (0, 0)),
            pl.BlockSpec((None, None, 1, d), lambda i: (k_shift, mod_row(i), 0, 0)),
            pl.BlockSpec((None, None, 1, d), lambda i: (k_scale, mod_row(i), 0, 0)),
            pl.BlockSpec((d, LANE), lambda i: (0, 0)),
            pl.BlockSpec((d, LANE), lambda i: (0, 0)),
            pl.BlockSpec((1, LANE), lambda i: (0, 0)),
        ],
        out_specs=(pl.BlockSpec((tt, d), lambda i: (i, 0)),
                   pl.BlockSpec((tt, d), lambda i: (i, 0)),
                   pl.BlockSpec((tt, LANE), lambda i: (i, 0)),
                   pl.BlockSpec((tt, LANE), lambda i: (i, 0)),
                   pl.BlockSpec((1, LANE), lambda i: (0, 0))),
        compiler_params=_cparams(1),
        name="moe_router",
    )(z, vec(g), vec(b), mods, mods, wr_hi, wr_lo, br)


def _gather_kernel(idx_ref, idx_next_ref, x_hbm, o_ref, buf_ref, sem):
    b = pl.program_id(0)
    nb = pl.num_programs(0)
    tb = idx_ref.shape[-1]
    slot = b % 2

    def issue(ids_ref, dst_slot):
        def body(r, carry):
            pltpu.make_async_copy(x_hbm.at[pl.ds(ids_ref[0, 0, r], 1)],
                                  buf_ref.at[dst_slot, pl.ds(r, 1)], sem.at[dst_slot]).start()
            return carry
        lax.fori_loop(0, tb, body, 0)

    @pl.when(b == 0)
    def _():
        issue(idx_ref, 0)

    @pl.when(b + 1 < nb)
    def _():
        issue(idx_next_ref, 1 - slot)

    pltpu.make_async_copy(x_hbm.at[pl.ds(0, tb)], buf_ref.at[slot], sem.at[slot]).wait()
    o_ref[...] = buf_ref[slot].astype(o_ref.dtype)


def _gather_rows(x, idx):
    nb, _, tb = idx.shape
    d = x.shape[1]
    return pl.pallas_call(
        _gather_kernel,
        out_shape=jax.ShapeDtypeStruct((nb * tb, d), BF16),
        grid=(nb,),
        in_specs=[
            pl.BlockSpec((1, 1, tb), lambda b: (b, 0, 0), memory_space=pltpu.SMEM),
            pl.BlockSpec((1, 1, tb), lambda b: (jnp.minimum(b + 1, nb - 1), 0, 0), memory_space=pltpu.SMEM),
            pl.BlockSpec(memory_space=pl.ANY),
        ],
        out_specs=pl.BlockSpec((tb, d), lambda b: (b, 0)),
        scratch_shapes=[pltpu.VMEM((2, tb, d), x.dtype), pltpu.SemaphoreType.DMA((2,))],
        compiler_params=_cparams(1),
        name="moe_gather",
    )(idx, idx, x)


def _moe_up_kernel(se_ref, sc_ref, sb_ref, ob_ref, oc_ref, snew_ref, nvalid_ref,
                   xs_ref, w1_ref, w3_ref, h_ref, w1c_ref, w3c_ref):
    s = pl.program_id(0)

    @pl.when(snew_ref[s] == 1)
    def _():
        w1c_ref[...] = w1_ref[...].astype(BF16)
        w3c_ref[...] = w3_ref[...].astype(BF16)

    @pl.when(s < nvalid_ref[0])
    def _():
        xb = xs_ref[...]
        h1 = jnp.dot(xb, w1c_ref[...], preferred_element_type=F32)
        h3 = jnp.dot(xb, w3c_ref[...], preferred_element_type=F32)
        h_ref[...] = (h1 * _sigmoid(h1) * h3).astype(h_ref.dtype)

    @pl.when(s >= nvalid_ref[0])
    def _():
        h_ref[...] = jnp.zeros_like(h_ref)


def _moe_up(tables, xs, w1, w3, layer, fc):
    slots, d = xs.shape
    f = w1.shape[3]
    nsteps = tables[0].shape[0]
    return pl.pallas_call(
        _moe_up_kernel,
        out_shape=jax.ShapeDtypeStruct((slots, f), BF16),
        grid_spec=pltpu.PrefetchScalarGridSpec(
            num_scalar_prefetch=7,
            grid=(nsteps,),
            in_specs=[
                pl.BlockSpec((MOE_TB, d), lambda s, se, sc, sb, ob, oc, sn, nv: (sb[s], 0)),
                pl.BlockSpec((None, None, d, fc), lambda s, se, sc, sb, ob, oc, sn, nv: (layer, se[s], 0, sc[s])),
                pl.BlockSpec((None, None, d, fc), lambda s, se, sc, sb, ob, oc, sn, nv: (layer, se[s], 0, sc[s])),
            ],
            out_specs=pl.BlockSpec((MOE_TB, fc), lambda s, se, sc, sb, ob, oc, sn, nv: (ob[s], oc[s])),
            scratch_shapes=[pltpu.VMEM((d, fc), BF16), pltpu.VMEM((d, fc), BF16)],
        ),
        compiler_params=_cparams(1),
        name="moe_up",
    )(*tables, xs, w1, w3)


def _moe_down_kernel(se_ref, sc_ref, sb_ref, ob_ref, oc_ref, snew_ref, nvalid_ref,
                     h_ref, w2_ref, y_ref, w2c_ref):
    s = pl.program_id(0)

    @pl.when(snew_ref[s] == 1)
    def _():
        w2c_ref[...] = w2_ref[...].astype(BF16)

    @pl.when(s < nvalid_ref[0])
    def _():
        y_ref[...] = jnp.dot(h_ref[...], w2c_ref[...], preferred_element_type=F32)

    @pl.when(s >= nvalid_ref[0])
    def _():
        y_ref[...] = jnp.zeros_like(y_ref)


def _moe_down(tables, h, w2, layer, nc):
    slots, f = h.shape
    d = w2.shape[3]
    nsteps = tables[0].shape[0]
    return pl.pallas_call(
        _moe_down_kernel,
        out_shape=jax.ShapeDtypeStruct((slots, d), F32),
        grid_spec=pltpu.PrefetchScalarGridSpec(
            num_scalar_prefetch=7,
            grid=(nsteps,),
            in_specs=[
                pl.BlockSpec((MOE_TB, f), lambda s, se, sc, sb, ob, oc, sn, nv: (sb[s], 0)),
                pl.BlockSpec((None, None, f, nc), lambda s, se, sc, sb, ob, oc, sn, nv: (layer, se[s], 0, sc[s])),
            ],
            out_specs=pl.BlockSpec((MOE_TB, nc), lambda s, se, sc, sb, ob, oc, sn, nv: (ob[s], oc[s])),
            scratch_shapes=[pltpu.VMEM((f, nc), BF16)],
        ),
        compiler_params=_cparams(1),
        name="moe_down",
    )(*tables, h, w2)


def _combine_kernel(dest_ref, dest_next_ref, y_hbm, x_ref, rw_ref, gate_ref, g_ref, b_ref, o_ref, ybuf_ref, sem, *, alpha):
    i = pl.program_id(0)
    nt = pl.num_programs(0)
    tt = x_ref.shape[0]
    slot = i % 2

    def issue(ids_ref, dst_slot):
        def body(r, carry):
            for k in range(TOP_K):
                pltpu.make_async_copy(y_hbm.at[pl.ds(ids_ref[0, 0, k * tt + r], 1)],
                                      ybuf_ref.at[dst_slot, k, pl.ds(r, 1)], sem.at[dst_slot]).start()
            return carry
        lax.fori_loop(0, tt, body, 0)

    @pl.when(i == 0)
    def _():
        issue(dest_ref, 0)

    @pl.when(i + 1 < nt)
    def _():
        issue(dest_next_ref, 1 - slot)

    for k in range(TOP_K):
        pltpu.make_async_copy(y_hbm.at[pl.ds(0, tt)], ybuf_ref.at[slot, k], sem.at[slot]).wait()

    rw = rw_ref[...]
    f = rw[:, 0:1] * ybuf_ref[slot, 0] + rw[:, 1:2] * ybuf_ref[slot, 1]
    z = alpha * x_ref[...] + gate_ref[...] * f
    mu = jnp.mean(z, axis=-1, keepdims=True)
    zc = z - mu
    var = jnp.mean(zc * zc, axis=-1, keepdims=True)
    o_ref[...] = zc * lax.rsqrt(var + LN_EPS) * g_ref[...] + b_ref[...]


def _combine(dest, y, x, rw, mods, k_gate, g, b, mod_row, alpha):
    n, d = x.shape
    tt = ROW_TILE
    vec = lambda v: v.reshape(1, d)
    return pl.pallas_call(
        functools.partial(_combine_kernel, alpha=alpha),
        out_shape=jax.ShapeDtypeStruct((n, d), F32),
        grid=(n // tt,),
        in_specs=[
            pl.BlockSpec((1, 1, TOP_K * tt), lambda i: (i, 0, 0), memory_space=pltpu.SMEM),
            pl.BlockSpec((1, 1, TOP_K * tt), lambda i: (jnp.minimum(i + 1, n // tt - 1), 0, 0),
                         memory_space=pltpu.SMEM),
            pl.BlockSpec(memory_space=pl.ANY),
            pl.BlockSpec((tt, d), lambda i: (i, 0)),
            pl.BlockSpec((tt, LANE), lambda i: (i, 0)),
            pl.BlockSpec((None, None, 1, d), lambda i: (k_gate, mod_row(i), 0, 0)),
            pl.BlockSpec((1, d), lambda i: (0, 0)),
            pl.BlockSpec((1, d), lambda i: (0, 0)),
        ],
        out_specs=pl.BlockSpec((tt, d), lambda i: (i, 0)),
        scratch_shapes=[pltpu.VMEM((2, TOP_K, tt, d), F32), pltpu.SemaphoreType.DMA((2,))],
        compiler_params=_cparams(1),
        name="moe_combine_ln",
    )(dest, dest, y, x, rw, mods, vec(g), vec(b))


def _dispatch_tables(flat_e, flat_rank, counts, n_chunks_up, n_chunks_down):
    nk = flat_e.shape[0]
    n_experts = counts.shape[0]
    tb = MOE_TB
    nb = nk // tb + n_experts
    eye = (flat_e[:, None] == jnp.arange(n_experts, dtype=jnp.int32)[None, :])
    blocks_e = (counts + tb - 1) // tb
    blk_end = jnp.cumsum(blocks_e)
    blk_start = blk_end - blocks_e
    dest = jnp.sum(jnp.where(eye, blk_start[None, :] * tb, 0), axis=1) + flat_rank
    slot_tok = jnp.zeros((nb * tb,), jnp.int32).at[dest].set(jnp.arange(nk, dtype=jnp.int32) // TOP_K)
    n_blocks = blk_end[-1]

    def steps(n_chunks):
        ns = nb * n_chunks
        per_e = blocks_e * n_chunks
        e_end = jnp.cumsum(per_e)
        e_start = e_end - per_e
        s = jnp.arange(ns, dtype=jnp.int32)
        nvalid = e_end[-1]
        sv = jnp.minimum(s, nvalid - 1)
        e = jnp.minimum(jnp.sum((sv[:, None] >= e_end[None, :]).astype(jnp.int32), axis=1), n_experts - 1)
        pick = e[:, None] == jnp.arange(n_experts, dtype=jnp.int32)[None, :]
        local = sv - jnp.sum(jnp.where(pick, e_start[None, :], 0), axis=1)
        be = jnp.maximum(jnp.sum(jnp.where(pick, blocks_e[None, :], 0), axis=1), 1)
        chunk = (local // be).astype(jnp.int32)
        blk = (jnp.sum(jnp.where(pick, blk_start[None, :], 0), axis=1) + local % be).astype(jnp.int32)
        key = e * n_chunks + chunk
        new = jnp.concatenate([jnp.ones((1,), jnp.int32), (key[1:] != key[:-1]).astype(jnp.int32)])
        extra = s - nvalid
        out_blk = jnp.where(s < nvalid, blk, n_blocks + extra // n_chunks).astype(jnp.int32)
        out_chunk = jnp.where(s < nvalid, chunk, extra % n_chunks).astype(jnp.int32)
        return (e, chunk, blk, out_blk, out_chunk, new, nvalid.reshape(1).astype(jnp.int32))

    return dest, slot_tok.reshape(nb, 1, tb), steps(n_chunks_up), steps(n_chunks_down), n_blocks


def _hier_moe_block(z, mods, ln1_g, ln1_b, ln2_g, ln2_b, rg_w, rg_b, re_w, re_b, w1, w3, w2, layer, mod_row, alpha):
    n, d = z.shape
    n_groups = rg_w.shape[1]
    n_experts = re_w.shape[1]
    f = w1.shape[3]
    wr = jnp.zeros((d, LANE), F32).at[:, :n_experts].set(re_w).at[:, n_experts:n_experts + n_groups].set(rg_w)
    br = jnp.zeros((1, LANE), F32).at[0, :n_experts].set(re_b).at[0, n_experts:n_experts + n_groups].set(rg_b)
    wr_hi = wr.astype(BF16)
    wr_lo = (wr - wr_hi.astype(F32)).astype(BF16)
    x, xm, ri, rw, cnt = _router(z, ln1_g, ln1_b, mods, 3, 4, wr_hi, wr_lo, br, mod_row, n_groups, n_experts)
    flat_e = ri[:, :TOP_K].reshape(-1)
    flat_rank = ri[:, TOP_K:2 * TOP_K].reshape(-1)
    counts = cnt[0, :n_experts].astype(jnp.int32)
    fc = min(512, f)
    nc = min(2048, d)
    dest, slot_tok, up_tables, down_tables, _ = _dispatch_tables(flat_e, flat_rank, counts, f // fc, d // nc)
    xs = _gather_rows(xm, slot_tok)
    h = _moe_up(up_tables, xs, w1, w3, layer, fc)
    y = _moe_down(down_tables, h, w2, layer, nc)
    tt = ROW_TILE
    dest_t = dest.reshape(n // tt, tt, TOP_K).transpose(0, 2, 1).reshape(n // tt, 1, TOP_K * tt)
    return _combine(dest_t, y, x, rw, mods, 5, ln2_g, ln2_b, mod_row, alpha)


def _pad_heads(v, nh, hp):
    lead = v.shape[:-1]
    hb = v.shape[-1] // nh
    v = v.reshape(lead + (nh, hb))
    v = jnp.pad(v, [(0, 0)] * len(lead) + [(0, 0), (0, hp - hb)])
    return v.reshape(lead + (nh * hp,))


def kernel(x_prompt, x_sample, state_lru, c, c_ctx, mod_w, mod_b, ln_g, ln_b, conv_w_in, conv_b_in, conv_w_dw, conv_b_dw, conv_ln_g, conv_ln_b, conv_w_out, conv_b_out, lru_w_in, lru_b_in, lru_w_sc, lru_b_sc, lru_w_a, lru_b_a, lru_w_x, lru_b_x, lru_lam, lru_w_out, lru_b_out, moe_rg_w, moe_rg_b, moe_re_w, moe_re_b, moe_w1, moe_w3, moe_w2):
    batch, seq, d = x_prompt.shape
    dec_batch, dec_seq, _ = x_sample.shape
    depth = mod_w.shape[0]
    n_p = batch * seq
    n_s = dec_batch * dec_seq
    alpha = (2 * depth) ** 0.25
    assert n_p % 512 == 0 and dec_seq % 512 == 0 and seq % ROW_TILE == 0 and ROW_TILE % GRID_W == 0

    x = jnp.concatenate([x_prompt.reshape(n_p, d), x_sample.reshape(n_s, d)], axis=0)

    n_rows = _round_up(dec_batch + 1, SUBLANE)
    cond = jnp.zeros((n_rows, d), F32).at[:dec_batch].set(c).at[dec_batch].set(c_ctx)
    m_all = _adaln(cond, mod_w, mod_b)
    m_all = m_all.reshape(depth, n_rows, 6, 1, d).transpose(0, 2, 1, 3, 4)
    mod_row = _mod_row_fn(ROW_TILE, n_p, dec_seq, dec_batch)

    new_states = []
    for l in range(depth):
        j = l // 2
        mods = m_all[l]
        if l % 2 == 0:
            u = _inproj(x, mods, 0, 1, conv_w_in[j].astype(BF16), conv_b_in[j], mod_row, glu=True)
            dw = (conv_w_dw[j], conv_b_dw[j], conv_ln_g[j], conv_ln_b[j])
            vp = _dwconv(u, 0, n_p, seq, *dw)
            vs = _dwconv(u, n_p, n_s, GRID_W, *dw)
            act = jnp.concatenate([vp, vs], axis=0)
            w_out = conv_w_out[j].astype(BF16)
            b_out = conv_b_out[j]
        else:
            nh = lru_w_a.shape[2]
            d_rnn = lru_w_sc.shape[2]
            hp = _round_up(d_rnn // nh, LANE)
            ph = functools.partial(_pad_heads, nh=nh, hp=hp)
            w_in = lru_w_in[j]
            w_in_p = jnp.concatenate([ph(w_in[:, :d_rnn]), ph(w_in[:, d_rnn:])], axis=1).astype(BF16)
            b_in = lru_b_in[j]
            b_in_p = jnp.concatenate([ph(b_in[:d_rnn]), ph(b_in[d_rnn:])])
            yx = _inproj(x, mods, 0, 1, w_in_p, b_in_p, mod_row, glu=False)

            def pad_sq(w):
                hb = w.shape[-1]
                return jnp.pad(w, [(0, 0), (0, 0), (0, hp - hb), (0, hp - hb)]).astype(BF16)

            rows = jnp.concatenate([
                ph(lru_w_sc[j]), ph(lru_b_sc[j])[None], ph(lru_b_a[j]), ph(lru_b_x[j]), ph(lru_lam[j])], axis=0)
            params = jnp.zeros((16, nh * hp), F32).at[:rows.shape[0]].set(rows)
            params = params.reshape(16, nh, hp).transpose(1, 0, 2)
            wa = pad_sq(lru_w_a[j])
            wx = pad_sq(lru_w_x[j])
            h0_p = jnp.zeros((batch, 2, nh * hp), F32)
            h0_s = ph(state_lru[:, j])
            mixed_p, fin_p = _lru_core(yx, 0, batch, seq, wa, wx, params, h0_p)
            mixed_s, _ = _lru_core(yx, n_p, dec_batch, dec_seq, wa, wx, params, h0_s)
            act = jnp.concatenate([mixed_p, mixed_s], axis=0)
            hb = d_rnn // nh
            new_states.append(fin_p.reshape(batch, 2, nh, hp)[..., :hb].reshape(batch, 2, d_rnn))
            w_out = jnp.pad(lru_w_out[j].reshape(nh, hb, d), [(0, 0), (0, hp - hb), (0, 0)]).reshape(nh * hp, d).astype(BF16)
            b_out = lru_b_out[j]
        z = _outproj(act, w_out, b_out, x, mods, 2, mod_row, alpha)
        x = _hier_moe_block(z, mods, ln_g[l, 0], ln_b[l, 0], ln_g[l, 1], ln_b[l, 1],
                            moe_rg_w[l], moe_rg_b[l], moe_re_w[l], moe_re_b[l],
                            moe_w1, moe_w3, moe_w2, l, mod_row, alpha)

    y_prompt = x[:n_p].reshape(batch, seq, d)
    y_sample = x[n_p:].reshape(dec_batch, dec_seq, d)
    new_state_lru = jnp.stack(new_states, axis=1).astype(x_prompt.dtype)
    return (y_prompt, y_sample, new_state_lru)
```

```python
import functools
import math

import jax
import jax.numpy as jnp
from jax import lax
from jax.experimental import pallas as pl
from jax.experimental.pallas import tpu as pltpu

F32 = jnp.float32
BF16 = jnp.bfloat16

GRID_W = 64
TOP_K = 2
LRU_C = 8.0
LN_EPS = 1e-5
LANE = 128
SUBLANE = 8
VMEM_LIMIT = 56 * 1024 * 1024
MOE_TB = 256
ROW_TILE = 256


def _sigmoid(x):
    return 1.0 / (1.0 + jnp.exp(-x))


def _pack_bf16_pair(a, b):
    hi = lax.bitcast_convert_type(a.astype(BF16).astype(F32), jnp.int32)
    lo = lax.bitcast_convert_type(b.astype(BF16).astype(F32), jnp.int32)
    return hi | lax.shift_right_logical(lo, 16)


def _unpack_hi(w):
    return lax.bitcast_convert_type(w & jnp.int32(-65536), F32)


def _unpack_lo(w):
    return lax.bitcast_convert_type(lax.shift_left(w, 16), F32)


def _round_up(x, m):
    return (x + m - 1) // m * m


def _cparams(n_axes):
    return pltpu.CompilerParams(
        dimension_semantics=("arbitrary",) * n_axes, vmem_limit_bytes=VMEM_LIMIT)


def _mod_row_fn(tile, n_prompt, dec_seq, ctx_row):
    def fn(i):
        t0 = i * tile
        return jnp.where(t0 < n_prompt, ctx_row, (t0 - n_prompt) // dec_seq)
    return fn


def _adaln_kernel(c_ref, w_ref, b_ref, o_ref):
    a = c_ref[...]
    a = (a * _sigmoid(a)).astype(BF16)
    o_ref[...] = jnp.dot(a, w_ref[...].astype(BF16), preferred_element_type=F32) + b_ref[...]


def _adaln(cond, mod_w, mod_b):
    depth, d, n6 = mod_w.shape
    r = cond.shape[0]
    tn = 512
    return pl.pallas_call(
        _adaln_kernel,
        out_shape=jax.ShapeDtypeStruct((depth, r, n6), F32),
        grid=(depth, n6 // tn),
        in_specs=[
            pl.BlockSpec((r, d), lambda l, j: (0, 0)),
            pl.BlockSpec((None, d, tn), lambda l, j: (l, 0, j)),
            pl.BlockSpec((None, 1, tn), lambda l, j: (l, 0, j)),
        ],
        out_specs=pl.BlockSpec((None, r, tn), lambda l, j: (l, 0, j)),
        compiler_params=_cparams(2),
        name="adaln",
    )(cond, mod_w, mod_b.reshape(depth, 1, n6))


def _row_sources(parts, tm, width_block, col_of):
    specs, starts = [], []
    t0 = 0
    for part in parts:
        nt = part.shape[0] // tm
        lo, hi = t0, t0 + nt

        def index(i, j, lo=lo, hi=hi):
            own = (i >= lo) & (i < hi)
            return (jnp.clip(i - lo, 0, hi - lo - 1), jnp.where(own, col_of(j), 0))

        specs.append(pl.BlockSpec((tm, width_block), index))
        starts.append(t0)
        t0 = hi
    return specs, starts


def _for_owner(starts, n_tiles, refs, fn):
    i = pl.program_id(0)
    bounds = list(starts) + [n_tiles]
    if len(refs) == 1:
        fn(refs[0])
        return
    for p, ref in enumerate(refs):
        pl.when((i >= bounds[p]) & (i < bounds[p + 1]))(functools.partial(fn, ref))


def _inproj_kernel(*refs, glu, n_src, starts, n_tiles):
    x_refs = refs[:n_src]
    sh_ref, sc_ref = refs[n_src:n_src + 2]
    rest = refs[n_src + 2:]
    if glu:
        wa_ref, wg_ref, ba_ref, bg_ref, o_ref, xs_ref = rest
    else:
        wa_ref, ba_ref, o_ref, xs_ref = rest

    def modulate(x_ref):
        xs_ref[...] = (x_ref[...] * (1.0 + sc_ref[...]) + sh_ref[...]).astype(BF16)

    @pl.when(pl.program_id(1) == 0)
    def _():
        _for_owner(starts, n_tiles, x_refs, modulate)

    xs = xs_ref[...]
    a = jnp.dot(xs, wa_ref[...], preferred_element_type=F32) + ba_ref[...]
    if glu:
        g = jnp.dot(xs, wg_ref[...], preferred_element_type=F32) + bg_ref[...]
        a = a * _sigmoid(g)
    o_ref[...] = a.astype(o_ref.dtype)


def _inproj(x_parts, mods, k_shift, k_scale, w, b, mod_row, *, glu, tm=512, tn=512):
    d = x_parts[0].shape[1]
    n = sum(p.shape[0] for p in x_parts)
    nw = w.shape[1]
    nout = nw // 2 if glu else nw
    if glu and len(x_parts) > 1:
        tn = tn // 2
    nj = nout // tn
    b2 = b.reshape(1, nw)
    row = lambda i, j: mod_row(i * (tm // ROW_TILE))
    x_specs, starts = _row_sources(x_parts, tm, d, lambda j: 0)
    in_specs = x_specs + [
        pl.BlockSpec((None, None, 1, d), lambda i, j: (k_shift, row(i, j), 0, 0)),
        pl.BlockSpec((None, None, 1, d), lambda i, j: (k_scale, row(i, j), 0, 0)),
    ]
    args = list(x_parts) + [mods, mods]
    if glu:
        in_specs += [
            pl.BlockSpec((d, tn), lambda i, j: (0, j)),
            pl.BlockSpec((d, tn), lambda i, j: (0, j + nj)),
            pl.BlockSpec((1, tn), lambda i, j: (0, j)),
            pl.BlockSpec((1, tn), lambda i, j: (0, j + nj)),
        ]
        args += [w, w, b2, b2]
    else:
        in_specs += [
            pl.BlockSpec((d, tn), lambda i, j: (0, j)),
            pl.BlockSpec((1, tn), lambda i, j: (0, j)),
        ]
        args += [w, b2]
    return pl.pallas_call(
        functools.partial(_inproj_kernel, glu=glu, n_src=len(x_parts), starts=tuple(starts), n_tiles=n // tm),
        out_shape=jax.ShapeDtypeStruct((n, nout), F32),
        grid=(n // tm, nj),
        in_specs=in_specs,
        out_specs=pl.BlockSpec((tm, tn), lambda i, j: (i, j)),
        scratch_shapes=[pltpu.VMEM((tm, d), BF16)],
        compiler_params=_cparams(2),
        name="inproj_glu" if glu else "inproj",
    )(*args)


CONV_ROWS = 64
CONV_CC = 256
CONV_GAP = 16
CONV_CHUNK = 1024


def _dwconv_kernel(u_ref, w_ref, bdw_ref, g_ref, b_ref, o_ref, sh_ref, v_ref, *, seg):
    rows, c = u_ref.shape
    width = w_ref.shape[0]
    half = width // 2
    nseg = rows // seg
    stride = seg + CONV_GAP
    pad_rows = nseg * stride + CONV_GAP
    chunk_w = sh_ref.shape[2]
    zeros_gap = jnp.zeros((CONV_GAP, chunk_w), F32)

    def lane_chunk(ci, carry):
        l0 = pl.multiple_of(ci * chunk_w, chunk_w)
        for s in range(nseg):
            sh_ref[0, s * stride:s * stride + CONV_GAP, :] = zeros_gap
            sh_ref[0, s * stride + CONV_GAP:(s + 1) * stride, :] = u_ref[s * seg:(s + 1) * seg, pl.ds(l0, chunk_w)]
        sh_ref[0, nseg * stride:pad_rows, :] = zeros_gap
        for j in range(1, SUBLANE):
            sh_ref[j, 0:pad_rows - SUBLANE, :] = sh_ref[0, j:j + pad_rows - SUBLANE, :]

        for rb in range(rows // CONV_ROWS):
            r0 = rb * CONV_ROWS
            s, within = divmod(r0, seg)
            base = s * stride + CONV_GAP + within - half

            def sub(cc, carry2, base=base, r0=r0):
                c0 = pl.multiple_of(cc * CONV_CC, CONV_CC)
                g0 = pl.multiple_of(l0 + c0, CONV_CC)
                acc = jnp.zeros((CONV_ROWS, CONV_CC), F32) + bdw_ref[:, pl.ds(g0, CONV_CC)]
                for k in range(width):
                    j = (base + k) % SUBLANE
                    q = base + k - j
                    acc = acc + w_ref[k:k + 1, pl.ds(g0, CONV_CC)] * sh_ref[j, q:q + CONV_ROWS, pl.ds(c0, CONV_CC)]
                v_ref[r0:r0 + CONV_ROWS, pl.ds(g0, CONV_CC)] = acc
                return carry2

            lax.fori_loop(0, chunk_w // CONV_CC, sub, 0)
        return carry

    lax.fori_loop(0, c // chunk_w, lane_chunk, 0)

    ln_rows = 32

    def ln_block(r, carry):
        q0 = pl.multiple_of(r * ln_rows, ln_rows)
        v = v_ref[pl.ds(q0, ln_rows), :]
        mu = jnp.mean(v, axis=-1, keepdims=True)
        vc = v - mu
        var = jnp.mean(vc * vc, axis=-1, keepdims=True)
        y = vc * lax.rsqrt(var + LN_EPS) * g_ref[...] + b_ref[...]
        o_ref[pl.ds(q0, ln_rows), :] = (y * _sigmoid(y)).astype(o_ref.dtype)
        return carry

    lax.fori_loop(0, rows // ln_rows, ln_block, 0)


def _dwconv(u, row0, nrows, seg, w_dw, b_dw, g_n, b_n):
    c = u.shape[1]
    width = w_dw.shape[0]
    tile = ROW_TILE
    assert tile % seg == 0 and seg % CONV_ROWS == 0 and CONV_GAP >= width // 2
    off = row0 // tile
    nseg = tile // seg
    vec = lambda a: a.reshape(1, c)
    return pl.pallas_call(
        functools.partial(_dwconv_kernel, seg=seg),
        out_shape=jax.ShapeDtypeStruct((nrows, c), BF16),
        grid=(nrows // tile,),
        in_specs=[
            pl.BlockSpec((tile, c), lambda i: (i + off, 0)),
            pl.BlockSpec((width, c), lambda i: (0, 0)),
            pl.BlockSpec((1, c), lambda i: (0, 0)),
            pl.BlockSpec((1, c), lambda i: (0, 0)),
            pl.BlockSpec((1, c), lambda i: (0, 0)),
        ],
        out_specs=pl.BlockSpec((tile, c), lambda i: (i, 0)),
        scratch_shapes=[
            pltpu.VMEM((SUBLANE, nseg * (seg + CONV_GAP) + CONV_GAP, min(CONV_CHUNK, c)), F32),
            pltpu.VMEM((tile, c), F32),
        ],
        compiler_params=_cparams(1),
        name="dwconv_seg%d" % seg,
    )(u, w_dw, vec(b_dw), vec(g_n), vec(b_n))


def _outproj_kernel(*refs, alpha, n_a, a_starts, n_x, x_starts, n_tiles):
    a_refs = refs[:n_a]
    w_ref, bias_ref = refs[n_a:n_a + 2]
    x_refs = refs[n_a + 2:n_a + 2 + n_x]
    gate_ref, o_ref = refs[n_a + 2 + n_x:]

    def project(a_ref):
        o_ref[...] = gate_ref[...] * (jnp.dot(a_ref[...], w_ref[...], preferred_element_type=F32) + bias_ref[...])

    def add_residual(x_ref):
        o_ref[...] += alpha * x_ref[...]

    _for_owner(a_starts, n_tiles, a_refs, project)
    _for_owner(x_starts, n_tiles, x_refs, add_residual)


def _outproj(a_parts, w, bias, x_parts, mods, k_gate, mod_row, alpha, *, tm=512, tn=512):
    kdim = a_parts[0].shape[1]
    n = sum(p.shape[0] for p in a_parts)
    d = w.shape[1]
    tn = min(tn, d)
    row = lambda i: mod_row(i * (tm // ROW_TILE))
    a_specs, a_starts = _row_sources(a_parts, tm, kdim, lambda j: 0)
    x_specs, x_starts = _row_sources(x_parts, tm, tn, lambda j: j)
    return pl.pallas_call(
        functools.partial(_outproj_kernel, alpha=alpha, n_a=len(a_parts), a_starts=tuple(a_starts),
                          n_x=len(x_parts), x_starts=tuple(x_starts), n_tiles=n // tm),
        out_shape=jax.ShapeDtypeStruct((n, d), F32),
        grid=(n // tm, d // tn),
        in_specs=a_specs + [
            pl.BlockSpec((kdim, tn), lambda i, j: (0, j)),
            pl.BlockSpec((1, tn), lambda i, j: (0, j)),
        ] + x_specs + [
            pl.BlockSpec((None, None, 1, tn), lambda i, j: (k_gate, row(i), 0, j)),
        ],
        out_specs=pl.BlockSpec((tm, tn), lambda i, j: (i, j)),
        compiler_params=_cparams(2),
        name="outproj",
    )(*a_parts, w, bias.reshape(1, d), *x_parts, mods)


def _lru_kernel(y_ref, xb_ref, wa_ref, wx_ref, p_ref, h0_ref, mixed_ref, fin_ref,
                a_ref, b_ref, hf_ref, hb_ref):
    t, hp = xb_ref.shape
    xb = xb_ref[...]
    p = p_ref[...]
    row = lax.broadcasted_iota(jnp.int32, (t, hp), 0)

    def from_above(x, s):
        return jnp.where(row >= s, pltpu.roll(x, s, 0), 0.0)

    def from_below(x, s):
        return jnp.where(row < t - s, pltpu.roll(x, t - s, 0), 0.0)

    xc = (p[0:1] * from_above(xb, 2) + p[1:2] * from_above(xb, 1) + p[2:3] * xb
          + p[3:4] * from_below(xb, 1) + p[4:5])
    xcb = xc.astype(BF16)
    for d in range(2):
        r = _sigmoid(jnp.dot(xcb, wa_ref[d], preferred_element_type=F32) + p[5 + d:6 + d])
        i = _sigmoid(jnp.dot(xcb, wx_ref[d], preferred_element_type=F32) + p[7 + d:8 + d])
        z = -p[9 + d:10 + d]
        softplus = jnp.maximum(z, 0.0) + jnp.log(1.0 + jnp.exp(-jnp.abs(z)))
        a = jnp.exp((-LRU_C) * softplus * r)
        a_ref[d] = a
        q = 1.0 - a * a
        b_ref[d] = (q * lax.rsqrt(jnp.maximum(q, 1e-30))) * (i * xc)

    grow = lax.broadcasted_iota(jnp.int32, (SUBLANE, hp), 0)
    ngroups = t // SUBLANE

    def group(g, carry):
        cf, cb = carry
        f0 = pl.multiple_of(g * SUBLANE, SUBLANE)
        af = a_ref[0, pl.ds(f0, SUBLANE), :]
        bf = b_ref[0, pl.ds(f0, SUBLANE), :]
        r0 = pl.multiple_of((ngroups - 1 - g) * SUBLANE, SUBLANE)
        ab = a_ref[1, pl.ds(r0, SUBLANE), :]
        bb = b_ref[1, pl.ds(r0, SUBLANE), :]
        for s in (1, 2, 4):
            mf = grow >= s
            bf = jnp.where(mf, af * pltpu.roll(bf, s, 0) + bf, bf)
            af = jnp.where(mf, af * pltpu.roll(af, s, 0), af)
            mb = grow < SUBLANE - s
            bb = jnp.where(mb, ab * pltpu.roll(bb, SUBLANE - s, 0) + bb, bb)
            ab = jnp.where(mb, ab * pltpu.roll(ab, SUBLANE - s, 0), ab)
        hf = af * cf + bf
        hb = ab * cb + bb
        hf_ref[pl.ds(f0, SUBLANE), :] = hf
        hb_ref[pl.ds(r0, SUBLANE), :] = hb
        cf = jnp.broadcast_to(hf[SUBLANE - 1:SUBLANE, :], (SUBLANE, hp))
        cb = jnp.broadcast_to(hb[0:1, :], (SUBLANE, hp))
        return cf, cb

    h0 = h0_ref[...]
    cf0 = jnp.broadcast_to(h0[0:1, :], (SUBLANE, hp))
    cb0 = jnp.broadcast_to(h0[1:2, :], (SUBLANE, hp))
    cf, cb = lax.fori_loop(0, ngroups, group, (cf0, cb0))
    fin_ref[0:1, :] = cf[0:1, :]
    fin_ref[1:2, :] = cb[0:1, :]

    y = y_ref[...]
    gelu = 0.5 * y * (1.0 + jnp.tanh(math.sqrt(2.0 / math.pi) * (y + 0.044715 * (y * y * y))))
    mixed_ref[...] = ((hf_ref[...] + hb_ref[...]) * gelu).astype(mixed_ref.dtype)


def _lru_core(yx, row0, n_seq, t, wa, wx, params, h0):
    nh, _, hp = params.shape
    dp = nh * hp
    off = row0 // t
    return pl.pallas_call(
        _lru_kernel,
        out_shape=(jax.ShapeDtypeStruct((n_seq * t, dp), BF16),
                   jax.ShapeDtypeStruct((n_seq, 2, dp), F32)),
        grid=(nh, n_seq),
        in_specs=[
            pl.BlockSpec((t, hp), lambda h, s: (s + off, h)),
            pl.BlockSpec((t, hp), lambda h, s: (s + off, h + nh)),
            pl.BlockSpec((2, None, hp, hp), lambda h, s: (0, h, 0, 0)),
            pl.BlockSpec((2, None, hp, hp), lambda h, s: (0, h, 0, 0)),
            pl.BlockSpec((None, 16, hp), lambda h, s: (h, 0, 0)),
            pl.BlockSpec((None, 2, hp), lambda h, s: (s, 0, h)),
        ],
        out_specs=(pl.BlockSpec((t, hp), lambda h, s: (s, h)),
                   pl.BlockSpec((None, 2, hp), lambda h, s: (s, 0, h))),
        scratch_shapes=[
            pltpu.VMEM((2, t, hp), F32), pltpu.VMEM((2, t, hp), F32),
            pltpu.VMEM((t, hp), F32), pltpu.VMEM((t, hp), F32),
        ],
        compiler_params=_cparams(2),
        name="lru_core_t%d" % t,
    )(yx, yx, wa, wx, params, h0)


def _router_kernel(z_ref, g_ref, b_ref, sh_ref, sc_ref, wh_ref, wl_ref, br_ref,
                   x_ref, xm_ref, ri_ref, rw_ref, cnt_ref, *, n_groups, n_experts):
    z = z_ref[...]
    mu = jnp.mean(z, axis=-1, keepdims=True)
    zc = z - mu
    var = jnp.mean(zc * zc, axis=-1, keepdims=True)
    x = zc * lax.rsqrt(var + LN_EPS) * g_ref[...] + b_ref[...]
    x_ref[...] = x
    xm = x * (1.0 + sc_ref[...]) + sh_ref[...]
    half = xm.shape[1] // 2
    xm_ref[...] = _pack_bf16_pair(xm[:, :half], xm[:, half:])
    xh = xm.astype(BF16)
    xl = (xm - xh.astype(F32)).astype(BF16)
    wh = wh_ref[...]
    logits = (jnp.dot(xh, wh, preferred_element_type=F32)
              + jnp.dot(xh, wl_ref[...], preferred_element_type=F32)
              + jnp.dot(xl, wh, preferred_element_type=F32)) + br_ref[...]
    tt = logits.shape[0]
    lane = lax.broadcasted_iota(jnp.int32, (tt, LANE), 1).astype(F32)
    neg = jnp.float32(-jnp.inf)
    big = jnp.float32(LANE)
    epg = n_experts // n_groups
    is_grp = (lane >= n_experts) & (lane < n_experts + n_groups)
    gl = jnp.where(is_grp, logits, neg)
    gmax = jnp.max(gl, axis=-1, keepdims=True)
    gsum = jnp.sum(jnp.where(is_grp, jnp.exp(gl - gmax), 0.0), axis=-1, keepdims=True)
    p_grp = 1.0 / gsum
    grp = jnp.min(jnp.where(gl == gmax, lane, big), axis=-1, keepdims=True) - n_experts
    in_grp = (lane >= grp * epg) & (lane < (grp + 1.0) * epg)
    el = jnp.where(in_grp, logits, neg)
    t1 = jnp.max(el, axis=-1, keepdims=True)
    i1 = jnp.min(jnp.where(el == t1, lane, big), axis=-1, keepdims=True)
    el2 = jnp.where(lane == i1, neg, el)
    t2 = jnp.max(el2, axis=-1, keepdims=True)
    i2 = jnp.min(jnp.where(el2 == t2, lane, big), axis=-1, keepdims=True)
    e2 = jnp.exp(t2 - t1)
    w1 = p_grp / (1.0 + e2)
    w2 = p_grp * e2 / (1.0 + e2)

    @pl.when(pl.program_id(0) == 0)
    def _():
        cnt_ref[...] = jnp.zeros_like(cnt_ref)

    sel1 = lane == i1
    sel2 = lane == i2
    onehot = jnp.where(sel1 | sel2, 1.0, 0.0)
    tr = lax.broadcasted_iota(jnp.int32, (tt, tt), 0)
    tc = lax.broadcasted_iota(jnp.int32, (tt, tt), 1)
    before = jnp.where(tc < tr, 1.0, 0.0).astype(BF16)
    prior = jnp.dot(before, onehot.astype(BF16), preferred_element_type=F32) + cnt_ref[...]
    r1 = jnp.sum(jnp.where(sel1, prior, 0.0), axis=-1, keepdims=True)
    r2 = jnp.sum(jnp.where(sel2, prior, 0.0), axis=-1, keepdims=True)
    cnt_ref[...] += jnp.sum(onehot, axis=0, keepdims=True)
    ri = jnp.where(lane == 0, i1, jnp.where(lane == 1, i2, jnp.where(lane == 2, r1, jnp.where(lane == 3, r2, 0.0))))
    ri_ref[...] = ri.astype(jnp.int32)
    rw_ref[...] = jnp.where(lane == 0, w1, jnp.where(lane == 1, w2, 0.0))


def _router(z, g, b, mods, k_shift, k_scale, wr_hi, wr_lo, br, mod_row, n_groups, n_experts):
    n, d = z.shape
    tt = ROW_TILE
    vec = lambda v: v.reshape(1, d)
    return pl.pallas_call(
        functools.partial(_router_kernel, n_groups=n_groups, n_experts=n_experts),
        out_shape=(jax.ShapeDtypeStruct((n, d), F32),
                   jax.ShapeDtypeStruct((n, d // 2), jnp.int32),
                   jax.ShapeDtypeStruct((n, LANE), jnp.int32),
                   jax.ShapeDtypeStruct((n, LANE), F32),
                   jax.ShapeDtypeStruct((1, LANE), F32)),
        grid=(n // tt,),
        in_specs=[
            pl.BlockSpec((tt, d), lambda i: (i, 0)),
            pl.BlockSpec((1, d), lambda i: (0, 0)),
            pl.BlockSpec((1, d), lambda i: (0, 0)),
            pl.BlockSpec((None, None, 1, d), lambda i: (k_shift, mod_row(i), 0, 0)),
            pl.BlockSpec((None, None, 1, d), lambda i: (k_scale, mod_row(i), 0, 0)),
            pl.BlockSpec((d, LANE), lambda i: (0, 0)),
            pl.BlockSpec((d, LANE), lambda i: (0, 0)),
            pl.BlockSpec((1, LANE), lambda i: (0, 0)),
        ],
        out_specs=(pl.BlockSpec((tt, d), lambda i: (i, 0)),
                   pl.BlockSpec((tt, d // 2), lambda i: (i, 0)),
                   pl.BlockSpec((tt, LANE), lambda i: (i, 0)),
                   pl.BlockSpec((tt, LANE), lambda i: (i, 0)),
                   pl.BlockSpec((1, LANE), lambda i: (0, 0))),
        compiler_params=_cparams(1),
        name="moe_router",
    )(z, vec(g), vec(b), mods, mods, wr_hi, wr_lo, br)


def _gather_kernel(idx_ref, idx_next_ref, x_hbm, o_ref, buf_ref, sem):
    b = pl.program_id(0)
    nb = pl.num_programs(0)
    tb = idx_ref.shape[-1]
    slot = b % 2

    def issue(ids_ref, dst_slot):
        def body(r, carry):
            pltpu.make_async_copy(x_hbm.at[pl.ds(ids_ref[0, 0, r], 1)],
                                  buf_ref.at[dst_slot, pl.ds(r, 1)], sem.at[dst_slot]).start()
            return carry
        lax.fori_loop(0, tb, body, 0)

    @pl.when(b == 0)
    def _():
        issue(idx_ref, 0)

    @pl.when(b + 1 < nb)
    def _():
        issue(idx_next_ref, 1 - slot)

    pltpu.make_async_copy(x_hbm.at[pl.ds(0, tb)], buf_ref.at[slot], sem.at[slot]).wait()
    w = buf_ref[slot]
    half = w.shape[1]
    o_ref[:, :half] = _unpack_hi(w).astype(o_ref.dtype)
    o_ref[:, half:] = _unpack_lo(w).astype(o_ref.dtype)


def _gather_rows(x, idx):
    nb, _, tb = idx.shape
    d = 2 * x.shape[1]
    return pl.pallas_call(
        _gather_kernel,
        out_shape=jax.ShapeDtypeStruct((nb * tb, d), BF16),
        grid=(nb,),
        in_specs=[
            pl.BlockSpec((1, 1, tb), lambda b: (b, 0, 0), memory_space=pltpu.SMEM),
            pl.BlockSpec((1, 1, tb), lambda b: (jnp.minimum(b + 1, nb - 1), 0, 0), memory_space=pltpu.SMEM),
            pl.BlockSpec(memory_space=pl.ANY),
        ],
        out_specs=pl.BlockSpec((tb, d), lambda b: (b, 0)),
        scratch_shapes=[pltpu.VMEM((2, tb, d // 2), x.dtype), pltpu.SemaphoreType.DMA((2,))],
        compiler_params=_cparams(1),
        name="moe_gather",
    )(idx, idx, x)


def _moe_up_kernel(sb_ref, ob_ref, oc_ref, snew_ref, sci_ref, nvalid_ref, ce_ref, cc_ref, nchg_ref,
                   xs_ref, w1_hbm, w3_hbm, h_ref, st1_ref, st3_ref, w1c_ref, w3c_ref, sem, *, layer):
    s = pl.program_id(0)
    fc = st1_ref.shape[1]

    def fetch(i):
        c0 = pl.multiple_of(cc_ref[i] * fc, fc)
        return (pltpu.make_async_copy(w1_hbm.at[layer, ce_ref[i], :, pl.ds(c0, fc)], st1_ref, sem.at[0]),
                pltpu.make_async_copy(w3_hbm.at[layer, ce_ref[i], :, pl.ds(c0, fc)], st3_ref, sem.at[1]))

    @pl.when(s == 0)
    def _():
        for cp in fetch(0):
            cp.start()

    @pl.when(snew_ref[s] == 1)
    def _():
        i = sci_ref[s]
        for cp in fetch(i):
            cp.wait()
        w1c_ref[...] = st1_ref[...].astype(BF16)
        w3c_ref[...] = st3_ref[...].astype(BF16)

        @pl.when(i + 1 < nchg_ref[0])
        def _():
            for cp in fetch(i + 1):
                cp.start()

    @pl.when(s < nvalid_ref[0])
    def _():
        xb = xs_ref[...]
        h1 = jnp.dot(xb, w1c_ref[...], preferred_element_type=F32)
        h3 = jnp.dot(xb, w3c_ref[...], preferred_element_type=F32)
        h_ref[...] = (h1 * _sigmoid(h1) * h3).astype(h_ref.dtype)

    @pl.when(s >= nvalid_ref[0])
    def _():
        h_ref[...] = jnp.zeros_like(h_ref)


def _moe_up(tables, xs, w1, w3, layer, fc):
    slots, d = xs.shape
    f = w1.shape[3]
    t = tables
    prefetch = (t["sb"], t["ob"], t["oc"], t["snew"], t["sci"], t["nvalid"], t["ce"], t["cc"], t["nchg"])
    nsteps = t["sb"].shape[0]
    return pl.pallas_call(
        functools.partial(_moe_up_kernel, layer=layer),
        out_shape=jax.ShapeDtypeStruct((slots, f), BF16),
        grid_spec=pltpu.PrefetchScalarGridSpec(
            num_scalar_prefetch=len(prefetch),
            grid=(nsteps,),
            in_specs=[
                pl.BlockSpec((MOE_TB, d), lambda s, sb, *_: (sb[s], 0)),
                pl.BlockSpec(memory_space=pl.ANY),
                pl.BlockSpec(memory_space=pl.ANY),
            ],
            out_specs=pl.BlockSpec((MOE_TB, fc), lambda s, sb, ob, oc, *_: (ob[s], oc[s])),
            scratch_shapes=[pltpu.VMEM((d, fc), F32), pltpu.VMEM((d, fc), F32),
                            pltpu.VMEM((d, fc), BF16), pltpu.VMEM((d, fc), BF16),
                            pltpu.SemaphoreType.DMA((2,))],
        ),
        compiler_params=_cparams(1),
        name="moe_up",
    )(*prefetch, xs, w1, w3)


def _moe_down_kernel(sb_ref, ob_ref, snew_ref, sci_ref, nvalid_ref, ce_ref, nchg_ref,
                     h_ref, w2_hbm, y_ref, st_ref, w2c_ref, sem, *, layer):
    s = pl.program_id(0)

    def fetch(i):
        return pltpu.make_async_copy(w2_hbm.at[layer, ce_ref[i]], st_ref, sem.at[0])

    @pl.when(s == 0)
    def _():
        fetch(0).start()

    @pl.when(snew_ref[s] == 1)
    def _():
        i = sci_ref[s]
        fetch(i).wait()
        w2c_ref[...] = st_ref[...].astype(BF16)

        @pl.when(i + 1 < nchg_ref[0])
        def _():
            fetch(i + 1).start()

    @pl.when(s < nvalid_ref[0])
    def _():
        y = jnp.dot(h_ref[...], w2c_ref[...], preferred_element_type=F32)
        half = y.shape[1] // 2
        y_ref[...] = _pack_bf16_pair(y[:, :half], y[:, half:])

    @pl.when(s >= nvalid_ref[0])
    def _():
        y_ref[...] = jnp.zeros_like(y_ref)


def _moe_down(tables, h, w2, layer):
    slots, f = h.shape
    d = w2.shape[3]
    t = tables
    prefetch = (t["sb"], t["ob"], t["snew"], t["sci"], t["nvalid"], t["ce"], t["nchg"])
    nsteps = t["sb"].shape[0]
    return pl.pallas_call(
        functools.partial(_moe_down_kernel, layer=layer),
        out_shape=jax.ShapeDtypeStruct((slots, d // 2), jnp.int32),
        grid_spec=pltpu.PrefetchScalarGridSpec(
            num_scalar_prefetch=len(prefetch),
            grid=(nsteps,),
            in_specs=[
                pl.BlockSpec((MOE_TB, f), lambda s, sb, *_: (sb[s], 0)),
                pl.BlockSpec(memory_space=pl.ANY),
            ],
            out_specs=pl.BlockSpec((MOE_TB, d // 2), lambda s, sb, ob, *_: (ob[s], 0)),
            scratch_shapes=[pltpu.VMEM((f, d), F32), pltpu.VMEM((f, d), BF16), pltpu.SemaphoreType.DMA((1,))],
        ),
        compiler_params=_cparams(1),
        name="moe_down",
    )(*prefetch, h, w2)


def _combine_kernel(dest_ref, dest_next_ref, y_hbm, x_ref, rw_ref, gate_ref, g_ref, b_ref, o_ref, ybuf_ref, sem, *, alpha):
    i = pl.program_id(0)
    nt = pl.num_programs(0)
    tt = x_ref.shape[0]
    slot = i % 2

    def issue(ids_ref, dst_slot):
        def body(r, carry):
            for k in range(TOP_K):
                pltpu.make_async_copy(y_hbm.at[pl.ds(ids_ref[0, 0, k * tt + r], 1)],
                                      ybuf_ref.at[dst_slot, k, pl.ds(r, 1)], sem.at[dst_slot]).start()
            return carry
        lax.fori_loop(0, tt, body, 0)

    @pl.when(i == 0)
    def _():
        issue(dest_ref, 0)

    @pl.when(i + 1 < nt)
    def _():
        issue(dest_next_ref, 1 - slot)

    for k in range(TOP_K):
        pltpu.make_async_copy(y_hbm.at[pl.ds(0, tt)], ybuf_ref.at[slot, k], sem.at[slot]).wait()

    rw = rw_ref[...]
    y0 = ybuf_ref[slot, 0]
    y1 = ybuf_ref[slot, 1]
    f = jnp.concatenate([rw[:, 0:1] * _unpack_hi(y0) + rw[:, 1:2] * _unpack_hi(y1),
                         rw[:, 0:1] * _unpack_lo(y0) + rw[:, 1:2] * _unpack_lo(y1)], axis=1)
    z = alpha * x_ref[...] + gate_ref[...] * f
    mu = jnp.mean(z, axis=-1, keepdims=True)
    zc = z - mu
    var = jnp.mean(zc * zc, axis=-1, keepdims=True)
    o_ref[...] = zc * lax.rsqrt(var + LN_EPS) * g_ref[...] + b_ref[...]


def _combine(dest, y, x, rw, mods, k_gate, g, b, mod_row, alpha, tile0, ntiles):
    d = x.shape[1]
    tt = ROW_TILE
    vec = lambda v: v.reshape(1, d)
    return pl.pallas_call(
        functools.partial(_combine_kernel, alpha=alpha),
        out_shape=jax.ShapeDtypeStruct((ntiles * tt, d), F32),
        grid=(ntiles,),
        in_specs=[
            pl.BlockSpec((1, 1, TOP_K * tt), lambda i: (i + tile0, 0, 0), memory_space=pltpu.SMEM),
            pl.BlockSpec((1, 1, TOP_K * tt), lambda i: (jnp.minimum(i + 1, ntiles - 1) + tile0, 0, 0),
                         memory_space=pltpu.SMEM),
            pl.BlockSpec(memory_space=pl.ANY),
            pl.BlockSpec((tt, d), lambda i: (i + tile0, 0)),
            pl.BlockSpec((tt, LANE), lambda i: (i + tile0, 0)),
            pl.BlockSpec((None, None, 1, d), lambda i: (k_gate, mod_row(i + tile0), 0, 0)),
            pl.BlockSpec((1, d), lambda i: (0, 0)),
            pl.BlockSpec((1, d), lambda i: (0, 0)),
        ],
        out_specs=pl.BlockSpec((tt, d), lambda i: (i, 0)),
        scratch_shapes=[pltpu.VMEM((2, TOP_K, tt, d // 2), jnp.int32), pltpu.SemaphoreType.DMA((2,))],
        compiler_params=_cparams(1),
        name="moe_combine_ln",
    )(dest, dest, y, x, rw, mods, vec(g), vec(b))


def _dispatch_tables(flat_e, flat_rank, counts, n_chunks_up, n_chunks_down):
    nk = flat_e.shape[0]
    n_experts = counts.shape[0]
    tb = MOE_TB
    nb = nk // tb + n_experts
    eye = (flat_e[:, None] == jnp.arange(n_experts, dtype=jnp.int32)[None, :])
    blocks_e = (counts + tb - 1) // tb
    blk_end = jnp.cumsum(blocks_e)
    blk_start = blk_end - blocks_e
    dest = jnp.sum(jnp.where(eye, blk_start[None, :] * tb, 0), axis=1) + flat_rank
    slot_tok = jnp.zeros((nb * tb,), jnp.int32).at[dest].set(jnp.arange(nk, dtype=jnp.int32) // TOP_K)
    n_blocks = blk_end[-1]

    def steps(n_chunks):
        ns = nb * n_chunks
        per_e = blocks_e * n_chunks
        e_end = jnp.cumsum(per_e)
        e_start = e_end - per_e
        s = jnp.arange(ns, dtype=jnp.int32)
        nvalid = e_end[-1]
        sv = jnp.minimum(s, nvalid - 1)
        e = jnp.minimum(jnp.sum((sv[:, None] >= e_end[None, :]).astype(jnp.int32), axis=1), n_experts - 1)
        pick = e[:, None] == jnp.arange(n_experts, dtype=jnp.int32)[None, :]
        local = sv - jnp.sum(jnp.where(pick, e_start[None, :], 0), axis=1)
        be = jnp.maximum(jnp.sum(jnp.where(pick, blocks_e[None, :], 0), axis=1), 1)
        chunk = (local // be).astype(jnp.int32)
        blk = (jnp.sum(jnp.where(pick, blk_start[None, :], 0), axis=1) + local % be).astype(jnp.int32)
        key = e * n_chunks + chunk
        new = jnp.concatenate([jnp.ones((1,), jnp.int32), (key[1:] != key[:-1]).astype(jnp.int32)])
        extra = s - nvalid
        out_blk = jnp.where(s < nvalid, blk, n_blocks + extra // n_chunks).astype(jnp.int32)
        out_chunk = jnp.where(s < nvalid, chunk, extra % n_chunks).astype(jnp.int32)
        n_pairs = n_experts * n_chunks
        pair = jnp.arange(n_pairs, dtype=jnp.int32)
        live = jnp.repeat(blocks_e > 0, n_chunks)
        pos = jnp.where(live, jnp.cumsum(live.astype(jnp.int32)) - 1, n_pairs)
        ce = jnp.zeros((n_pairs,), jnp.int32).at[pos].set(pair // n_chunks, mode='drop')
        cc = jnp.zeros((n_pairs,), jnp.int32).at[pos].set(pair % n_chunks, mode='drop')
        as1 = lambda v: v.reshape(1).astype(jnp.int32)
        return dict(sb=blk, ob=out_blk, oc=out_chunk, snew=new, sci=(jnp.cumsum(new) - 1).astype(jnp.int32),
                    nvalid=as1(nvalid), ce=ce, cc=cc, nchg=as1(jnp.sum(live)))

    return dest, slot_tok.reshape(nb, 1, tb), steps(n_chunks_up), steps(n_chunks_down), n_blocks


def _hier_moe_block(z, mods, ln1_g, ln1_b, ln2_g, ln2_b, rg_w, rg_b, re_w, re_b, w1, w3, w2, layer, mod_row, alpha,
                    out_splits):
    n, d = z.shape
    n_groups = rg_w.shape[1]
    n_experts = re_w.shape[1]
    f = w1.shape[3]
    wr = jnp.zeros((d, LANE), F32).at[:, :n_experts].set(re_w).at[:, n_experts:n_experts + n_groups].set(rg_w)
    br = jnp.zeros((1, LANE), F32).at[0, :n_experts].set(re_b).at[0, n_experts:n_experts + n_groups].set(rg_b)
    wr_hi = wr.astype(BF16)
    wr_lo = (wr - wr_hi.astype(F32)).astype(BF16)
    x, xm, ri, rw, cnt = _router(z, ln1_g, ln1_b, mods, 3, 4, wr_hi, wr_lo, br, mod_row, n_groups, n_experts)
    flat_e = ri[:, :TOP_K].reshape(-1)
    flat_rank = ri[:, TOP_K:2 * TOP_K].reshape(-1)
    counts = cnt[0, :n_experts].astype(jnp.int32)
    fc = min(512, f)
    dest, slot_tok, up_tables, down_tables, _ = _dispatch_tables(flat_e, flat_rank, counts, f // fc, 1)
    xs = _gather_rows(xm, slot_tok)
    h = _moe_up(up_tables, xs, w1, w3, layer, fc)
    y = _moe_down(down_tables, h, w2, layer)
    tt = ROW_TILE
    dest_t = dest.reshape(n // tt, tt, TOP_K).transpose(0, 2, 1).reshape(n // tt, 1, TOP_K * tt)
    return [_combine(dest_t, y, x, rw, mods, 5, ln2_g, ln2_b, mod_row, alpha, t0, nt) for t0, nt in out_splits]


def _pad_heads(v, nh, hp):
    lead = v.shape[:-1]
    hb = v.shape[-1] // nh
    v = v.reshape(lead + (nh, hb))
    v = jnp.pad(v, [(0, 0)] * len(lead) + [(0, 0), (0, hp - hb)])
    return v.reshape(lead + (nh * hp,))


def kernel(x_prompt, x_sample, state_lru, c, c_ctx, mod_w, mod_b, ln_g, ln_b, conv_w_in, conv_b_in, conv_w_dw, conv_b_dw, conv_ln_g, conv_ln_b, conv_w_out, conv_b_out, lru_w_in, lru_b_in, lru_w_sc, lru_b_sc, lru_w_a, lru_b_a, lru_w_x, lru_b_x, lru_lam, lru_w_out, lru_b_out, moe_rg_w, moe_rg_b, moe_re_w, moe_re_b, moe_w1, moe_w3, moe_w2):
    batch, seq, d = x_prompt.shape
    dec_batch, dec_seq, _ = x_sample.shape
    depth = mod_w.shape[0]
    n_p = batch * seq
    n_s = dec_batch * dec_seq
    alpha = (2 * depth) ** 0.25
    assert n_p % 512 == 0 and dec_seq % 512 == 0 and seq % ROW_TILE == 0 and ROW_TILE % GRID_W == 0

    x_parts = [x_prompt.reshape(n_p, d), x_sample.reshape(n_s, d)]

    n_rows = _round_up(dec_batch + 1, SUBLANE)
    cond = jnp.zeros((n_rows, d), F32).at[:dec_batch].set(c).at[dec_batch].set(c_ctx)
    m_all = _adaln(cond, mod_w, mod_b)
    m_all = m_all.reshape(depth, n_rows, 6, 1, d).transpose(0, 2, 1, 3, 4)
    mod_row = _mod_row_fn(ROW_TILE, n_p, dec_seq, dec_batch)

    new_states = []
    for l in range(depth):
        j = l // 2
        mods = m_all[l]
        if l % 2 == 0:
            u = _inproj(x_parts, mods, 0, 1, conv_w_in[j].astype(BF16), conv_b_in[j], mod_row, glu=True)
            dw = (conv_w_dw[j], conv_b_dw[j], conv_ln_g[j], conv_ln_b[j])
            act_parts = [_dwconv(u, 0, n_p, seq, *dw), _dwconv(u, n_p, n_s, GRID_W, *dw)]
            w_out = conv_w_out[j].astype(BF16)
            b_out = conv_b_out[j]
        else:
            nh = lru_w_a.shape[2]
            d_rnn = lru_w_sc.shape[2]
            hp = _round_up(d_rnn // nh, LANE)
            ph = functools.partial(_pad_heads, nh=nh, hp=hp)
            w_in_p = _pad_heads(lru_w_in[j].astype(BF16), 2 * nh, hp)
            b_in_p = _pad_heads(lru_b_in[j], 2 * nh, hp)
            yx = _inproj(x_parts, mods, 0, 1, w_in_p, b_in_p, mod_row, glu=False)

            def pad_sq(w):
                hb = w.shape[-1]
                return jnp.pad(w, [(0, 0), (0, 0), (0, hp - hb), (0, hp - hb)]).astype(BF16)

            rows = jnp.concatenate([
                ph(lru_w_sc[j]), ph(lru_b_sc[j])[None], ph(lru_b_a[j]), ph(lru_b_x[j]), ph(lru_lam[j])], axis=0)
            params = jnp.zeros((16, nh * hp), F32).at[:rows.shape[0]].set(rows)
            params = params.reshape(16, nh, hp).transpose(1, 0, 2)
            wa = pad_sq(lru_w_a[j])
            wx = pad_sq(lru_w_x[j])
            h0_p = jnp.zeros((batch, 2, nh * hp), F32)
            h0_s = ph(state_lru[:, j])
            mixed_p, fin_p = _lru_core(yx, 0, batch, seq, wa, wx, params, h0_p)
            mixed_s, _ = _lru_core(yx, n_p, dec_batch, dec_seq, wa, wx, params, h0_s)
            act_parts = [mixed_p, mixed_s]
            hb = d_rnn // nh
            new_states.append(fin_p.reshape(batch, 2, nh, hp)[..., :hb].reshape(batch, 2, d_rnn))
            w_out = jnp.pad(lru_w_out[j].reshape(nh, hb, d), [(0, 0), (0, hp - hb), (0, 0)]).reshape(nh * hp, d).astype(BF16)
            b_out = lru_b_out[j]
        z = _outproj(act_parts, w_out, b_out, x_parts, mods, 2, mod_row, alpha)
        tiles_p, tiles_s = n_p // ROW_TILE, n_s // ROW_TILE
        splits = [(0, tiles_p), (tiles_p, tiles_s)] if l == depth - 1 else [(0, tiles_p + tiles_s)]
        x_parts = _hier_moe_block(z, mods, ln_g[l, 0], ln_b[l, 0], ln_g[l, 1], ln_b[l, 1],
                                  moe_rg_w[l], moe_rg_b[l], moe_re_w[l], moe_re_b[l],
                                  moe_w1, moe_w3, moe_w2, l, mod_row, alpha, splits)

    y_prompt = x_parts[0].reshape(batch, seq, d)
    y_sample = x_parts[1].reshape(dec_batch, dec_seq, d)
    new_state_lru = jnp.stack(new_states, axis=1).astype(x_prompt.dtype)
    return (y_prompt, y_sample, new_state_lru)
```

```python
import functools
import math

import jax
import jax.numpy as jnp
from jax import lax
from jax.experimental import pallas as pl
from jax.experimental.pallas import tpu as pltpu

F32 = jnp.float32
BF16 = jnp.bfloat16

GRID_W = 64
TOP_K = 2
LRU_C = 8.0
LN_EPS = 1e-5
LANE = 128
SUBLANE = 8
VMEM_LIMIT = 56 * 1024 * 1024
MOE_TB = 256
ROW_TILE = 256
DMA_ISSUE_UNROLL = 8


def _sigmoid(x):
    return 1.0 / (1.0 + jnp.exp(-x))


def _pack_bf16_pair(a, b):
    hi = lax.bitcast_convert_type(a.astype(BF16).astype(F32), jnp.int32)
    lo = lax.bitcast_convert_type(b.astype(BF16).astype(F32), jnp.int32)
    return hi | lax.shift_right_logical(lo, 16)


def _unpack_hi(w):
    return lax.bitcast_convert_type(w & jnp.int32(-65536), F32)


def _unpack_lo(w):
    return lax.bitcast_convert_type(lax.shift_left(w, 16), F32)


def _round_up(x, m):
    return (x + m - 1) // m * m


def _cparams(n_axes):
    return pltpu.CompilerParams(
        dimension_semantics=("arbitrary",) * n_axes, vmem_limit_bytes=VMEM_LIMIT)


def _mod_row_fn(tile, n_prompt, dec_seq, ctx_row):
    def fn(i):
        t0 = i * tile
        return jnp.where(t0 < n_prompt, ctx_row, (t0 - n_prompt) // dec_seq)
    return fn


def _adaln_kernel(c_ref, w_ref, b_ref, o_ref):
    a = c_ref[...]
    a = (a * _sigmoid(a)).astype(BF16)
    o_ref[...] = jnp.dot(a, w_ref[...].astype(BF16), preferred_element_type=F32) + b_ref[...]


def _adaln(cond, mod_w, mod_b):
    depth, d, n6 = mod_w.shape
    r = cond.shape[0]
    tn = 512
    return pl.pallas_call(
        _adaln_kernel,
        out_shape=jax.ShapeDtypeStruct((depth, r, n6), F32),
        grid=(depth, n6 // tn),
        in_specs=[
            pl.BlockSpec((r, d), lambda l, j: (0, 0)),
            pl.BlockSpec((None, d, tn), lambda l, j: (l, 0, j)),
            pl.BlockSpec((None, 1, tn), lambda l, j: (l, 0, j)),
        ],
        out_specs=pl.BlockSpec((None, r, tn), lambda l, j: (l, 0, j)),
        compiler_params=_cparams(2),
        name="adaln",
    )(cond, mod_w, mod_b.reshape(depth, 1, n6))


def _row_sources(parts, tm, width_block, col_of, single_buffer=False):
    mode = dict(pipeline_mode=pl.Buffered(1)) if single_buffer else {}
    specs, starts = [], []
    t0 = 0
    for part in parts:
        nt = part.shape[0] // tm
        lo, hi = t0, t0 + nt

        def index(i, j, lo=lo, hi=hi):
            own = (i >= lo) & (i < hi)
            return (jnp.clip(i - lo, 0, hi - lo - 1), jnp.where(own, col_of(j), 0))

        specs.append(pl.BlockSpec((tm, width_block), index, **mode))
        starts.append(t0)
        t0 = hi
    return specs, starts


def _for_owner(starts, n_tiles, refs, fn):
    i = pl.program_id(0)
    bounds = list(starts) + [n_tiles]
    if len(refs) == 1:
        fn(refs[0])
        return
    for p, ref in enumerate(refs):
        pl.when((i >= bounds[p]) & (i < bounds[p + 1]))(functools.partial(fn, ref))


def _inproj_kernel(*refs, glu, n_src, starts, n_tiles):
    x_refs = refs[:n_src]
    sh_ref, sc_ref = refs[n_src:n_src + 2]
    rest = refs[n_src + 2:]
    if glu:
        wa_ref, wg_ref, ba_ref, bg_ref, o_ref, xs_ref = rest
    else:
        wa_ref, ba_ref, o_ref, xs_ref = rest

    def modulate(x_ref):
        xs_ref[...] = (x_ref[...] * (1.0 + sc_ref[...]) + sh_ref[...]).astype(BF16)

    @pl.when(pl.program_id(1) == 0)
    def _():
        _for_owner(starts, n_tiles, x_refs, modulate)

    xs = xs_ref[...]
    a = jnp.dot(xs, wa_ref[...], preferred_element_type=F32) + ba_ref[...]
    if glu:
        g = jnp.dot(xs, wg_ref[...], preferred_element_type=F32) + bg_ref[...]
        a = a * _sigmoid(g)
    o_ref[...] = a.astype(o_ref.dtype)


def _inproj(x_parts, mods, k_shift, k_scale, w, b, mod_row, *, glu, tm=512, tn=512):
    d = x_parts[0].shape[1]
    n = sum(p.shape[0] for p in x_parts)
    nw = w.shape[1]
    nout = nw // 2 if glu else nw
    tn = min(tn, nout)
    nj = nout // tn
    b2 = b.reshape(1, nw)
    row = lambda i, j: mod_row(i * (tm // ROW_TILE))
    x_specs, starts = _row_sources(x_parts, tm, d, lambda j: 0, single_buffer=len(x_parts) > 1)
    in_specs = x_specs + [
        pl.BlockSpec((None, None, 1, d), lambda i, j: (k_shift, row(i, j), 0, 0)),
        pl.BlockSpec((None, None, 1, d), lambda i, j: (k_scale, row(i, j), 0, 0)),
    ]
    args = list(x_parts) + [mods, mods]
    if glu:
        in_specs += [
            pl.BlockSpec((d, tn), lambda i, j: (0, j)),
            pl.BlockSpec((d, tn), lambda i, j: (0, j + nj)),
            pl.BlockSpec((1, tn), lambda i, j: (0, j)),
            pl.BlockSpec((1, tn), lambda i, j: (0, j + nj)),
        ]
        args += [w, w, b2, b2]
    else:
        in_specs += [
            pl.BlockSpec((d, tn), lambda i, j: (0, j)),
            pl.BlockSpec((1, tn), lambda i, j: (0, j)),
        ]
        args += [w, b2]
    return pl.pallas_call(
        functools.partial(_inproj_kernel, glu=glu, n_src=len(x_parts), starts=tuple(starts), n_tiles=n // tm),
        out_shape=jax.ShapeDtypeStruct((n, nout), F32),
        grid=(n // tm, nj),
        in_specs=in_specs,
        out_specs=pl.BlockSpec((tm, tn), lambda i, j: (i, j)),
        scratch_shapes=[pltpu.VMEM((tm, d), BF16)],
        compiler_params=_cparams(2),
        name="inproj_glu" if glu else "inproj",
    )(*args)


CONV_ROWS = 64
CONV_CC = 256
CONV_GAP = 16
CONV_CHUNK = 1024


def _dwconv_kernel(u_ref, w_ref, bdw_ref, g_ref, b_ref, o_ref, sh_ref, v_ref, *, seg):
    rows, c = u_ref.shape
    width = w_ref.shape[0]
    half = width // 2
    nseg = rows // seg
    stride = seg + CONV_GAP
    pad_rows = nseg * stride + CONV_GAP
    chunk_w = sh_ref.shape[2]
    zeros_gap = jnp.zeros((CONV_GAP, chunk_w), F32)

    def lane_chunk(ci, carry):
        l0 = pl.multiple_of(ci * chunk_w, chunk_w)
        for s in range(nseg):
            sh_ref[0, s * stride:s * stride + CONV_GAP, :] = zeros_gap
            sh_ref[0, s * stride + CONV_GAP:(s + 1) * stride, :] = u_ref[s * seg:(s + 1) * seg, pl.ds(l0, chunk_w)]
        sh_ref[0, nseg * stride:pad_rows, :] = zeros_gap
        for j in range(1, SUBLANE):
            sh_ref[j, 0:pad_rows - SUBLANE, :] = sh_ref[0, j:j + pad_rows - SUBLANE, :]

        for rb in range(rows // CONV_ROWS):
            r0 = rb * CONV_ROWS
            s, within = divmod(r0, seg)
            base = s * stride + CONV_GAP + within - half

            def sub(cc, carry2, base=base, r0=r0):
                c0 = pl.multiple_of(cc * CONV_CC, CONV_CC)
                g0 = pl.multiple_of(l0 + c0, CONV_CC)
                acc = jnp.zeros((CONV_ROWS, CONV_CC), F32) + bdw_ref[:, pl.ds(g0, CONV_CC)]
                for k in range(width):
                    j = (base + k) % SUBLANE
                    q = base + k - j
                    acc = acc + w_ref[k:k + 1, pl.ds(g0, CONV_CC)] * sh_ref[j, q:q + CONV_ROWS, pl.ds(c0, CONV_CC)]
                v_ref[r0:r0 + CONV_ROWS, pl.ds(g0, CONV_CC)] = acc
                return carry2

            lax.fori_loop(0, chunk_w // CONV_CC, sub, 0)
        return carry

    lax.fori_loop(0, c // chunk_w, lane_chunk, 0)

    ln_rows = 32

    def ln_block(r, carry):
        q0 = pl.multiple_of(r * ln_rows, ln_rows)
        v = v_ref[pl.ds(q0, ln_rows), :]
        mu = jnp.mean(v, axis=-1, keepdims=True)
        vc = v - mu
        var = jnp.mean(vc * vc, axis=-1, keepdims=True)
        y = vc * lax.rsqrt(var + LN_EPS) * g_ref[...] + b_ref[...]
        o_ref[pl.ds(q0, ln_rows), :] = (y * _sigmoid(y)).astype(o_ref.dtype)
        return carry

    lax.fori_loop(0, rows // ln_rows, ln_block, 0)


def _dwconv(u, row0, nrows, seg, w_dw, b_dw, g_n, b_n):
    c = u.shape[1]
    width = w_dw.shape[0]
    tile = ROW_TILE
    assert tile % seg == 0 and seg % CONV_ROWS == 0 and CONV_GAP >= width // 2
    off = row0 // tile
    nseg = tile // seg
    vec = lambda a: a.reshape(1, c)
    return pl.pallas_call(
        functools.partial(_dwconv_kernel, seg=seg),
        out_shape=jax.ShapeDtypeStruct((nrows, c), BF16),
        grid=(nrows // tile,),
        in_specs=[
            pl.BlockSpec((tile, c), lambda i: (i + off, 0)),
            pl.BlockSpec((width, c), lambda i: (0, 0)),
            pl.BlockSpec((1, c), lambda i: (0, 0)),
            pl.BlockSpec((1, c), lambda i: (0, 0)),
            pl.BlockSpec((1, c), lambda i: (0, 0)),
        ],
        out_specs=pl.BlockSpec((tile, c), lambda i: (i, 0)),
        scratch_shapes=[
            pltpu.VMEM((SUBLANE, nseg * (seg + CONV_GAP) + CONV_GAP, min(CONV_CHUNK, c)), F32),
            pltpu.VMEM((tile, c), F32),
        ],
        compiler_params=_cparams(1),
        name="dwconv_seg%d" % seg,
    )(u, w_dw, vec(b_dw), vec(g_n), vec(b_n))


def _outproj_kernel(*refs, alpha, n_a, a_starts, n_x, x_starts, n_tiles):
    a_refs = refs[:n_a]
    w_ref, bias_ref = refs[n_a:n_a + 2]
    x_refs = refs[n_a + 2:n_a + 2 + n_x]
    gate_ref, o_ref = refs[n_a + 2 + n_x:]

    def project(a_ref):
        o_ref[...] = gate_ref[...] * (jnp.dot(a_ref[...], w_ref[...], preferred_element_type=F32) + bias_ref[...])

    def add_residual(x_ref):
        o_ref[...] += alpha * x_ref[...]

    _for_owner(a_starts, n_tiles, a_refs, project)
    _for_owner(x_starts, n_tiles, x_refs, add_residual)


def _outproj(a_parts, w, bias, x_parts, mods, k_gate, mod_row, alpha, *, tm=512, tn=512):
    kdim = a_parts[0].shape[1]
    n = sum(p.shape[0] for p in a_parts)
    d = w.shape[1]
    tn = min(tn, d)
    row = lambda i: mod_row(i * (tm // ROW_TILE))
    a_specs, a_starts = _row_sources(a_parts, tm, kdim, lambda j: 0)
    x_specs, x_starts = _row_sources(x_parts, tm, tn, lambda j: j)
    return pl.pallas_call(
        functools.partial(_outproj_kernel, alpha=alpha, n_a=len(a_parts), a_starts=tuple(a_starts),
                          n_x=len(x_parts), x_starts=tuple(x_starts), n_tiles=n // tm),
        out_shape=jax.ShapeDtypeStruct((n, d), F32),
        grid=(n // tm, d // tn),
        in_specs=a_specs + [
            pl.BlockSpec((kdim, tn), lambda i, j: (0, j)),
            pl.BlockSpec((1, tn), lambda i, j: (0, j)),
        ] + x_specs + [
            pl.BlockSpec((None, None, 1, tn), lambda i, j: (k_gate, row(i), 0, j)),
        ],
        out_specs=pl.BlockSpec((tm, tn), lambda i, j: (i, j)),
        compiler_params=_cparams(2),
        name="outproj",
    )(*a_parts, w, bias.reshape(1, d), *x_parts, mods)


def _lru_kernel(y_ref, xb_ref, wa_ref, wx_ref, p_ref, h0_ref, mixed_ref, fin_ref,
                a_ref, b_ref, hf_ref, hb_ref):
    t, hp = xb_ref.shape
    xb = xb_ref[...]
    p = p_ref[...]
    row = lax.broadcasted_iota(jnp.int32, (t, hp), 0)

    def from_above(x, s):
        return jnp.where(row >= s, pltpu.roll(x, s, 0), 0.0)

    def from_below(x, s):
        return jnp.where(row < t - s, pltpu.roll(x, t - s, 0), 0.0)

    xc = (p[0:1] * from_above(xb, 2) + p[1:2] * from_above(xb, 1) + p[2:3] * xb
          + p[3:4] * from_below(xb, 1) + p[4:5])
    xcb = xc.astype(BF16)
    for d in range(2):
        r = _sigmoid(jnp.dot(xcb, wa_ref[d], preferred_element_type=F32) + p[5 + d:6 + d])
        i = _sigmoid(jnp.dot(xcb, wx_ref[d], preferred_element_type=F32) + p[7 + d:8 + d])
        z = -p[9 + d:10 + d]
        softplus = jnp.maximum(z, 0.0) + jnp.log(1.0 + jnp.exp(-jnp.abs(z)))
        a = jnp.exp((-LRU_C) * softplus * r)
        a_ref[d] = a
        q = 1.0 - a * a
        b_ref[d] = (q * lax.rsqrt(jnp.maximum(q, 1e-30))) * (i * xc)

    grow = lax.broadcasted_iota(jnp.int32, (SUBLANE, hp), 0)
    ngroups = t // SUBLANE

    def group(g, carry):
        cf, cb = carry
        f0 = pl.multiple_of(g * SUBLANE, SUBLANE)
        af = a_ref[0, pl.ds(f0, SUBLANE), :]
        bf = b_ref[0, pl.ds(f0, SUBLANE), :]
        r0 = pl.multiple_of((ngroups - 1 - g) * SUBLANE, SUBLANE)
        ab = a_ref[1, pl.ds(r0, SUBLANE), :]
        bb = b_ref[1, pl.ds(r0, SUBLANE), :]
        for s in (1, 2, 4):
            mf = grow >= s
            bf = jnp.where(mf, af * pltpu.roll(bf, s, 0) + bf, bf)
            af = jnp.where(mf, af * pltpu.roll(af, s, 0), af)
            mb = grow < SUBLANE - s
            bb = jnp.where(mb, ab * pltpu.roll(bb, SUBLANE - s, 0) + bb, bb)
            ab = jnp.where(mb, ab * pltpu.roll(ab, SUBLANE - s, 0), ab)
        hf = af * cf + bf
        hb = ab * cb + bb
        hf_ref[pl.ds(f0, SUBLANE), :] = hf
        hb_ref[pl.ds(r0, SUBLANE), :] = hb
        cf = jnp.broadcast_to(hf[SUBLANE - 1:SUBLANE, :], (SUBLANE, hp))
        cb = jnp.broadcast_to(hb[0:1, :], (SUBLANE, hp))
        return cf, cb

    h0 = h0_ref[...]
    cf0 = jnp.broadcast_to(h0[0:1, :], (SUBLANE, hp))
    cb0 = jnp.broadcast_to(h0[1:2, :], (SUBLANE, hp))
    cf, cb = lax.fori_loop(0, ngroups, group, (cf0, cb0))
    fin_ref[0:1, :] = cf[0:1, :]
    fin_ref[1:2, :] = cb[0:1, :]

    y = y_ref[...]
    gelu = 0.5 * y * (1.0 + jnp.tanh(math.sqrt(2.0 / math.pi) * (y + 0.044715 * (y * y * y))))
    mixed_ref[...] = ((hf_ref[...] + hb_ref[...]) * gelu).astype(mixed_ref.dtype)


def _lru_core(yx, row0, n_seq, t, wa, wx, params, h0):
    nh, _, hp = params.shape
    dp = nh * hp
    off = row0 // t
    return pl.pallas_call(
        _lru_kernel,
        out_shape=(jax.ShapeDtypeStruct((n_seq * t, dp), BF16),
                   jax.ShapeDtypeStruct((n_seq, 2, dp), F32)),
        grid=(nh, n_seq),
        in_specs=[
            pl.BlockSpec((t, hp), lambda h, s: (s + off, h)),
            pl.BlockSpec((t, hp), lambda h, s: (s + off, h + nh)),
            pl.BlockSpec((2, None, hp, hp), lambda h, s: (0, h, 0, 0)),
            pl.BlockSpec((2, None, hp, hp), lambda h, s: (0, h, 0, 0)),
            pl.BlockSpec((None, 16, hp), lambda h, s: (h, 0, 0)),
            pl.BlockSpec((None, 2, hp), lambda h, s: (s, 0, h)),
        ],
        out_specs=(pl.BlockSpec((t, hp), lambda h, s: (s, h)),
                   pl.BlockSpec((None, 2, hp), lambda h, s: (s, 0, h))),
        scratch_shapes=[
            pltpu.VMEM((2, t, hp), F32), pltpu.VMEM((2, t, hp), F32),
            pltpu.VMEM((t, hp), F32), pltpu.VMEM((t, hp), F32),
        ],
        compiler_params=_cparams(2),
        name="lru_core_t%d" % t,
    )(yx, yx, wa, wx, params, h0)


def _router_kernel(z_ref, g_ref, b_ref, sh_ref, sc_ref, wh_ref, wl_ref, br_ref,
                   x_ref, xm_ref, ri_ref, rw_ref, cnt_ref, *, n_groups, n_experts):
    z = z_ref[...]
    mu = jnp.mean(z, axis=-1, keepdims=True)
    zc = z - mu
    var = jnp.mean(zc * zc, axis=-1, keepdims=True)
    x = zc * lax.rsqrt(var + LN_EPS) * g_ref[...] + b_ref[...]
    x_ref[...] = x
    xm = x * (1.0 + sc_ref[...]) + sh_ref[...]
    half = xm.shape[1] // 2
    xm_ref[...] = _pack_bf16_pair(xm[:, :half], xm[:, half:])
    xh = xm.astype(BF16)
    xl = (xm - xh.astype(F32)).astype(BF16)
    wh = wh_ref[...]
    logits = (jnp.dot(xh, wh, preferred_element_type=F32)
              + jnp.dot(xh, wl_ref[...], preferred_element_type=F32)
              + jnp.dot(xl, wh, preferred_element_type=F32)) + br_ref[...]
    tt = logits.shape[0]
    lane = lax.broadcasted_iota(jnp.int32, (tt, LANE), 1).astype(F32)
    neg = jnp.float32(-jnp.inf)
    big = jnp.float32(LANE)
    epg = n_experts // n_groups
    is_grp = (lane >= n_experts) & (lane < n_experts + n_groups)
    gl = jnp.where(is_grp, logits, neg)
    gmax = jnp.max(gl, axis=-1, keepdims=True)
    gsum = jnp.sum(jnp.where(is_grp, jnp.exp(gl - gmax), 0.0), axis=-1, keepdims=True)
    p_grp = 1.0 / gsum
    grp = jnp.min(jnp.where(gl == gmax, lane, big), axis=-1, keepdims=True) - n_experts
    in_grp = (lane >= grp * epg) & (lane < (grp + 1.0) * epg)
    el = jnp.where(in_grp, logits, neg)
    t1 = jnp.max(el, axis=-1, keepdims=True)
    i1 = jnp.min(jnp.where(el == t1, lane, big), axis=-1, keepdims=True)
    el2 = jnp.where(lane == i1, neg, el)
    t2 = jnp.max(el2, axis=-1, keepdims=True)
    i2 = jnp.min(jnp.where(el2 == t2, lane, big), axis=-1, keepdims=True)
    e2 = jnp.exp(t2 - t1)
    w1 = p_grp / (1.0 + e2)
    w2 = p_grp * e2 / (1.0 + e2)

    @pl.when(pl.program_id(0) == 0)
    def _():
        cnt_ref[...] = jnp.zeros_like(cnt_ref)

    sel1 = lane == i1
    sel2 = lane == i2
    onehot = jnp.where(sel1 | sel2, 1.0, 0.0)
    tr = lax.broadcasted_iota(jnp.int32, (tt, tt), 0)
    tc = lax.broadcasted_iota(jnp.int32, (tt, tt), 1)
    before = jnp.where(tc < tr, 1.0, 0.0).astype(BF16)
    prior = jnp.dot(before, onehot.astype(BF16), preferred_element_type=F32) + cnt_ref[...]
    r1 = jnp.sum(jnp.where(sel1, prior, 0.0), axis=-1, keepdims=True)
    r2 = jnp.sum(jnp.where(sel2, prior, 0.0), axis=-1, keepdims=True)
    cnt_ref[...] += jnp.sum(onehot, axis=0, keepdims=True)
    ri = jnp.where(lane == 0, i1, jnp.where(lane == 1, i2, jnp.where(lane == 2, r1, jnp.where(lane == 3, r2, 0.0))))
    ri_ref[...] = ri.astype(jnp.int32)
    rw_ref[...] = jnp.where(lane == 0, w1, jnp.where(lane == 1, w2, 0.0))


def _router(z, g, b, mods, k_shift, k_scale, wr_hi, wr_lo, br, mod_row, n_groups, n_experts):
    n, d = z.shape
    tt = ROW_TILE
    vec = lambda v: v.reshape(1, d)
    return pl.pallas_call(
        functools.partial(_router_kernel, n_groups=n_groups, n_experts=n_experts),
        out_shape=(jax.ShapeDtypeStruct((n, d), F32),
                   jax.ShapeDtypeStruct((n, d // 2), jnp.int32),
                   jax.ShapeDtypeStruct((n, LANE), jnp.int32),
                   jax.ShapeDtypeStruct((n, LANE), F32),
                   jax.ShapeDtypeStruct((1, LANE), F32)),
        grid=(n // tt,),
        in_specs=[
            pl.BlockSpec((tt, d), lambda i: (i, 0)),
            pl.BlockSpec((1, d), lambda i: (0, 0)),
            pl.BlockSpec((1, d), lambda i: (0, 0)),
            pl.BlockSpec((None, None, 1, d), lambda i: (k_shift, mod_row(i), 0, 0)),
            pl.BlockSpec((None, None, 1, d), lambda i: (k_scale, mod_row(i), 0, 0)),
            pl.BlockSpec((d, LANE), lambda i: (0, 0)),
            pl.BlockSpec((d, LANE), lambda i: (0, 0)),
            pl.BlockSpec((1, LANE), lambda i: (0, 0)),
        ],
        out_specs=(pl.BlockSpec((tt, d), lambda i: (i, 0)),
                   pl.BlockSpec((tt, d // 2), lambda i: (i, 0)),
                   pl.BlockSpec((tt, LANE), lambda i: (i, 0)),
                   pl.BlockSpec((tt, LANE), lambda i: (i, 0)),
                   pl.BlockSpec((1, LANE), lambda i: (0, 0))),
        compiler_params=_cparams(1),
        name="moe_router",
    )(z, vec(g), vec(b), mods, mods, wr_hi, wr_lo, br)


def _gather_kernel(idx_ref, idx_next_ref, x_hbm, o_ref, buf_ref, sem):
    b = pl.program_id(0)
    nb = pl.num_programs(0)
    tb = idx_ref.shape[-1]
    slot = b % 2

    def issue(ids_ref, dst_slot):
        def body(r, carry):
            pltpu.make_async_copy(x_hbm.at[pl.ds(ids_ref[0, 0, r], 1)],
                                  buf_ref.at[dst_slot, pl.ds(r, 1)], sem.at[dst_slot]).start()
            return carry
        lax.fori_loop(0, tb, body, 0, unroll=DMA_ISSUE_UNROLL)

    @pl.when(b == 0)
    def _():
        issue(idx_ref, 0)

    @pl.when(b + 1 < nb)
    def _():
        issue(idx_next_ref, 1 - slot)

    pltpu.make_async_copy(x_hbm.at[pl.ds(0, tb)], buf_ref.at[slot], sem.at[slot]).wait()
    w = buf_ref[slot]
    half = w.shape[1]
    o_ref[:, :half] = _unpack_hi(w).astype(o_ref.dtype)
    o_ref[:, half:] = _unpack_lo(w).astype(o_ref.dtype)


def _gather_rows(x, idx):
    nb, _, tb = idx.shape
    d = 2 * x.shape[1]
    return pl.pallas_call(
        _gather_kernel,
        out_shape=jax.ShapeDtypeStruct((nb * tb, d), BF16),
        grid=(nb,),
        in_specs=[
            pl.BlockSpec((1, 1, tb), lambda b: (b, 0, 0), memory_space=pltpu.SMEM),
            pl.BlockSpec((1, 1, tb), lambda b: (jnp.minimum(b + 1, nb - 1), 0, 0), memory_space=pltpu.SMEM),
            pl.BlockSpec(memory_space=pl.ANY),
        ],
        out_specs=pl.BlockSpec((tb, d), lambda b: (b, 0)),
        scratch_shapes=[pltpu.VMEM((2, tb, d // 2), x.dtype), pltpu.SemaphoreType.DMA((2,))],
        compiler_params=_cparams(1),
        name="moe_gather",
    )(idx, idx, x)


def _moe_up_kernel(sb_ref, ob_ref, oc_ref, snew_ref, sci_ref, nvalid_ref, ce_ref, cc_ref, nchg_ref,
                   xs_ref, w1_hbm, w3_hbm, h_ref, st1_ref, st3_ref, w1c_ref, w3c_ref, sem, *, layer):
    s = pl.program_id(0)
    fc = st1_ref.shape[1]

    def fetch(i):
        c0 = pl.multiple_of(cc_ref[i] * fc, fc)
        return (pltpu.make_async_copy(w1_hbm.at[layer, ce_ref[i], :, pl.ds(c0, fc)], st1_ref, sem.at[0]),
                pltpu.make_async_copy(w3_hbm.at[layer, ce_ref[i], :, pl.ds(c0, fc)], st3_ref, sem.at[1]))

    @pl.when(s == 0)
    def _():
        for cp in fetch(0):
            cp.start()

    @pl.when(snew_ref[s] == 1)
    def _():
        i = sci_ref[s]
        for cp in fetch(i):
            cp.wait()
        w1c_ref[...] = st1_ref[...].astype(BF16)
        w3c_ref[...] = st3_ref[...].astype(BF16)

        @pl.when(i + 1 < nchg_ref[0])
        def _():
            for cp in fetch(i + 1):
                cp.start()

    @pl.when(s < nvalid_ref[0])
    def _():
        xb = xs_ref[...]
        h1 = jnp.dot(xb, w1c_ref[...], preferred_element_type=F32)
        h3 = jnp.dot(xb, w3c_ref[...], preferred_element_type=F32)
        h_ref[...] = (h1 * _sigmoid(h1) * h3).astype(h_ref.dtype)

    @pl.when(s >= nvalid_ref[0])
    def _():
        h_ref[...] = jnp.zeros_like(h_ref)


def _moe_up(tables, xs, w1, w3, layer, fc):
    slots, d = xs.shape
    f = w1.shape[3]
    t = tables
    prefetch = (t["sb"], t["ob"], t["oc"], t["snew"], t["sci"], t["nvalid"], t["ce"], t["cc"], t["nchg"])
    nsteps = t["sb"].shape[0]
    return pl.pallas_call(
        functools.partial(_moe_up_kernel, layer=layer),
        out_shape=jax.ShapeDtypeStruct((slots, f), BF16),
        grid_spec=pltpu.PrefetchScalarGridSpec(
            num_scalar_prefetch=len(prefetch),
            grid=(nsteps,),
            in_specs=[
                pl.BlockSpec((MOE_TB, d), lambda s, sb, *_: (sb[s], 0)),
                pl.BlockSpec(memory_space=pl.ANY),
                pl.BlockSpec(memory_space=pl.ANY),
            ],
            out_specs=pl.BlockSpec((MOE_TB, fc), lambda s, sb, ob, oc, *_: (ob[s], oc[s])),
            scratch_shapes=[pltpu.VMEM((d, fc), F32), pltpu.VMEM((d, fc), F32),
                            pltpu.VMEM((d, fc), BF16), pltpu.VMEM((d, fc), BF16),
                            pltpu.SemaphoreType.DMA((2,))],
        ),
        compiler_params=_cparams(1),
        name="moe_up",
    )(*prefetch, xs, w1, w3)


def _moe_down_kernel(sb_ref, ob_ref, snew_ref, sci_ref, nvalid_ref, ce_ref, nchg_ref,
                     h_ref, w2_hbm, y_ref, st_ref, w2c_ref, sem, *, layer):
    s = pl.program_id(0)

    def fetch(i):
        return pltpu.make_async_copy(w2_hbm.at[layer, ce_ref[i]], st_ref, sem.at[0])

    @pl.when(s == 0)
    def _():
        fetch(0).start()

    @pl.when(snew_ref[s] == 1)
    def _():
        i = sci_ref[s]
        fetch(i).wait()
        w2c_ref[...] = st_ref[...].astype(BF16)

        @pl.when(i + 1 < nchg_ref[0])
        def _():
            fetch(i + 1).start()

    @pl.when(s < nvalid_ref[0])
    def _():
        y = jnp.dot(h_ref[...], w2c_ref[...], preferred_element_type=F32)
        half = y.shape[1] // 2
        y_ref[...] = _pack_bf16_pair(y[:, :half], y[:, half:])

    @pl.when(s >= nvalid_ref[0])
    def _():
        y_ref[...] = jnp.zeros_like(y_ref)


def _moe_down(tables, h, w2, layer):
    slots, f = h.shape
    d = w2.shape[3]
    t = tables
    prefetch = (t["sb"], t["ob"], t["snew"], t["sci"], t["nvalid"], t["ce"], t["nchg"])
    nsteps = t["sb"].shape[0]
    return pl.pallas_call(
        functools.partial(_moe_down_kernel, layer=layer),
        out_shape=jax.ShapeDtypeStruct((slots, d // 2), jnp.int32),
        grid_spec=pltpu.PrefetchScalarGridSpec(
            num_scalar_prefetch=len(prefetch),
            grid=(nsteps,),
            in_specs=[
                pl.BlockSpec((MOE_TB, f), lambda s, sb, *_: (sb[s], 0)),
                pl.BlockSpec(memory_space=pl.ANY),
            ],
            out_specs=pl.BlockSpec((MOE_TB, d // 2), lambda s, sb, ob, *_: (ob[s], 0)),
            scratch_shapes=[pltpu.VMEM((f, d), F32), pltpu.VMEM((f, d), BF16), pltpu.SemaphoreType.DMA((1,))],
        ),
        compiler_params=_cparams(1),
        name="moe_down",
    )(*prefetch, h, w2)


def _combine_kernel(dest_ref, dest_next_ref, y_hbm, x_ref, rw_ref, gate_ref, g_ref, b_ref, o_ref, ybuf_ref, sem, *, alpha):
    i = pl.program_id(0)
    nt = pl.num_programs(0)
    tt = x_ref.shape[0]
    slot = i % 2

    def issue(ids_ref, dst_slot):
        def body(r, carry):
            for k in range(TOP_K):
                pltpu.make_async_copy(y_hbm.at[pl.ds(ids_ref[0, 0, k * tt + r], 1)],
                                      ybuf_ref.at[dst_slot, k, pl.ds(r, 1)], sem.at[dst_slot]).start()
            return carry
        lax.fori_loop(0, tt, body, 0, unroll=DMA_ISSUE_UNROLL)

    @pl.when(i == 0)
    def _():
        issue(dest_ref, 0)

    @pl.when(i + 1 < nt)
    def _():
        issue(dest_next_ref, 1 - slot)

    for k in range(TOP_K):
        pltpu.make_async_copy(y_hbm.at[pl.ds(0, tt)], ybuf_ref.at[slot, k], sem.at[slot]).wait()

    rw = rw_ref[...]
    y0 = ybuf_ref[slot, 0]
    y1 = ybuf_ref[slot, 1]
    f = jnp.concatenate([rw[:, 0:1] * _unpack_hi(y0) + rw[:, 1:2] * _unpack_hi(y1),
                         rw[:, 0:1] * _unpack_lo(y0) + rw[:, 1:2] * _unpack_lo(y1)], axis=1)
    z = alpha * x_ref[...] + gate_ref[...] * f
    mu = jnp.mean(z, axis=-1, keepdims=True)
    zc = z - mu
    var = jnp.mean(zc * zc, axis=-1, keepdims=True)
    o_ref[...] = zc * lax.rsqrt(var + LN_EPS) * g_ref[...] + b_ref[...]


def _combine(dest, y, x, rw, mods, k_gate, g, b, mod_row, alpha, tile0, ntiles):
    d = x.shape[1]
    tt = ROW_TILE
    vec = lambda v: v.reshape(1, d)
    return pl.pallas_call(
        functools.partial(_combine_kernel, alpha=alpha),
        out_shape=jax.ShapeDtypeStruct((ntiles * tt, d), F32),
        grid=(ntiles,),
        in_specs=[
            pl.BlockSpec((1, 1, TOP_K * tt), lambda i: (i + tile0, 0, 0), memory_space=pltpu.SMEM),
            pl.BlockSpec((1, 1, TOP_K * tt), lambda i: (jnp.minimum(i + 1, ntiles - 1) + tile0, 0, 0),
                         memory_space=pltpu.SMEM),
            pl.BlockSpec(memory_space=pl.ANY),
            pl.BlockSpec((tt, d), lambda i: (i + tile0, 0)),
            pl.BlockSpec((tt, LANE), lambda i: (i + tile0, 0)),
            pl.BlockSpec((None, None, 1, d), lambda i: (k_gate, mod_row(i + tile0), 0, 0)),
            pl.BlockSpec((1, d), lambda i: (0, 0)),
            pl.BlockSpec((1, d), lambda i: (0, 0)),
        ],
        out_specs=pl.BlockSpec((tt, d), lambda i: (i, 0)),
        scratch_shapes=[pltpu.VMEM((2, TOP_K, tt, d // 2), jnp.int32), pltpu.SemaphoreType.DMA((2,))],
        compiler_params=_cparams(1),
        name="moe_combine_ln",
    )(dest, dest, y, x, rw, mods, vec(g), vec(b))


def _dispatch_tables(flat_e, flat_rank, counts, n_chunks_up, n_chunks_down):
    nk = flat_e.shape[0]
    n_experts = counts.shape[0]
    tb = MOE_TB
    nb = nk // tb + n_experts
    eye = (flat_e[:, None] == jnp.arange(n_experts, dtype=jnp.int32)[None, :])
    blocks_e = (counts + tb - 1) // tb
    blk_end = jnp.cumsum(blocks_e)
    blk_start = blk_end - blocks_e
    dest = jnp.sum(jnp.where(eye, blk_start[None, :] * tb, 0), axis=1) + flat_rank
    slot_tok = jnp.zeros((nb * tb,), jnp.int32).at[dest].set(jnp.arange(nk, dtype=jnp.int32) // TOP_K)
    n_blocks = blk_end[-1]

    def steps(n_chunks):
        ns = nb * n_chunks
        per_e = blocks_e * n_chunks
        e_end = jnp.cumsum(per_e)
        e_start = e_end - per_e
        s = jnp.arange(ns, dtype=jnp.int32)
        nvalid = e_end[-1]
        sv = jnp.minimum(s, nvalid - 1)
        e = jnp.minimum(jnp.sum((sv[:, None] >= e_end[None, :]).astype(jnp.int32), axis=1), n_experts - 1)
        pick = e[:, None] == jnp.arange(n_experts, dtype=jnp.int32)[None, :]
        local = sv - jnp.sum(jnp.where(pick, e_start[None, :], 0), axis=1)
        be = jnp.maximum(jnp.sum(jnp.where(pick, blocks_e[None, :], 0), axis=1), 1)
        chunk = (local // be).astype(jnp.int32)
        blk = (jnp.sum(jnp.where(pick, blk_start[None, :], 0), axis=1) + local % be).astype(jnp.int32)
        key = e * n_chunks + chunk
        new = jnp.concatenate([jnp.ones((1,), jnp.int32), (key[1:] != key[:-1]).astype(jnp.int32)])
        extra = s - nvalid
        out_blk = jnp.where(s < nvalid, blk, n_blocks + extra // n_chunks).astype(jnp.int32)
        out_chunk = jnp.where(s < nvalid, chunk, extra % n_chunks).astype(jnp.int32)
        n_pairs = n_experts * n_chunks
        pair = jnp.arange(n_pairs, dtype=jnp.int32)
        live = jnp.repeat(blocks_e > 0, n_chunks)
        pos = jnp.where(live, jnp.cumsum(live.astype(jnp.int32)) - 1, n_pairs)
        ce = jnp.zeros((n_pairs,), jnp.int32).at[pos].set(pair // n_chunks, mode='drop')
        cc = jnp.zeros((n_pairs,), jnp.int32).at[pos].set(pair % n_chunks, mode='drop')
        as1 = lambda v: v.reshape(1).astype(jnp.int32)
        return dict(sb=blk, ob=out_blk, oc=out_chunk, snew=new, sci=(jnp.cumsum(new) - 1).astype(jnp.int32),
                    nvalid=as1(nvalid), ce=ce, cc=cc, nchg=as1(jnp.sum(live)))

    return dest, slot_tok.reshape(nb, 1, tb), steps(n_chunks_up), steps(n_chunks_down), n_blocks


def _hier_moe_block(z, mods, ln1_g, ln1_b, ln2_g, ln2_b, rg_w, rg_b, re_w, re_b, w1, w3, w2, layer, mod_row, alpha,
                    out_splits):
    n, d = z.shape
    n_groups = rg_w.shape[1]
    n_experts = re_w.shape[1]
    f = w1.shape[3]
    wr = jnp.zeros((d, LANE), F32).at[:, :n_experts].set(re_w).at[:, n_experts:n_experts + n_groups].set(rg_w)
    br = jnp.zeros((1, LANE), F32).at[0, :n_experts].set(re_b).at[0, n_experts:n_experts + n_groups].set(rg_b)
    wr_hi = wr.astype(BF16)
    wr_lo = (wr - wr_hi.astype(F32)).astype(BF16)
    x, xm, ri, rw, cnt = _router(z, ln1_g, ln1_b, mods, 3, 4, wr_hi, wr_lo, br, mod_row, n_groups, n_experts)
    flat_e = ri[:, :TOP_K].reshape(-1)
    flat_rank = ri[:, TOP_K:2 * TOP_K].reshape(-1)
    counts = cnt[0, :n_experts].astype(jnp.int32)
    fc = min(512, f)
    dest, slot_tok, up_tables, down_tables, _ = _dispatch_tables(flat_e, flat_rank, counts, f // fc, 1)
    xs = _gather_rows(xm, slot_tok)
    h = _moe_up(up_tables, xs, w1, w3, layer, fc)
    y = _moe_down(down_tables, h, w2, layer)
    tt = ROW_TILE
    dest_t = dest.reshape(n // tt, tt, TOP_K).transpose(0, 2, 1).reshape(n // tt, 1, TOP_K * tt)
    return [_combine(dest_t, y, x, rw, mods, 5, ln2_g, ln2_b, mod_row, alpha, t0, nt) for t0, nt in out_splits]


def _pad_heads(v, nh, hp):
    lead = v.shape[:-1]
    hb = v.shape[-1] // nh
    v = v.reshape(lead + (nh, hb))
    v = jnp.pad(v, [(0, 0)] * len(lead) + [(0, 0), (0, hp - hb)])
    return v.reshape(lead + (nh * hp,))


def kernel(x_prompt, x_sample, state_lru, c, c_ctx, mod_w, mod_b, ln_g, ln_b, conv_w_in, conv_b_in, conv_w_dw, conv_b_dw, conv_ln_g, conv_ln_b, conv_w_out, conv_b_out, lru_w_in, lru_b_in, lru_w_sc, lru_b_sc, lru_w_a, lru_b_a, lru_w_x, lru_b_x, lru_lam, lru_w_out, lru_b_out, moe_rg_w, moe_rg_b, moe_re_w, moe_re_b, moe_w1, moe_w3, moe_w2):
    batch, seq, d = x_prompt.shape
    dec_batch, dec_seq, _ = x_sample.shape
    depth = mod_w.shape[0]
    n_p = batch * seq
    n_s = dec_batch * dec_seq
    alpha = (2 * depth) ** 0.25
    assert n_p % 512 == 0 and dec_seq % 512 == 0 and seq % ROW_TILE == 0 and ROW_TILE % GRID_W == 0

    x_parts = [x_prompt.reshape(n_p, d), x_sample.reshape(n_s, d)]

    n_rows = _round_up(dec_batch + 1, SUBLANE)
    cond = jnp.zeros((n_rows, d), F32).at[:dec_batch].set(c).at[dec_batch].set(c_ctx)
    m_all = _adaln(cond, mod_w, mod_b)
    m_all = m_all.reshape(depth, n_rows, 6, 1, d).transpose(0, 2, 1, 3, 4)
    mod_row = _mod_row_fn(ROW_TILE, n_p, dec_seq, dec_batch)

    new_states = []
    for l in range(depth):
        j = l // 2
        mods = m_all[l]
        if l % 2 == 0:
            u = _inproj(x_parts, mods, 0, 1, conv_w_in[j].astype(BF16), conv_b_in[j], mod_row, glu=True)
            dw = (conv_w_dw[j], conv_b_dw[j], conv_ln_g[j], conv_ln_b[j])
            act_parts = [_dwconv(u, 0, n_p, seq, *dw), _dwconv(u, n_p, n_s, GRID_W, *dw)]
            w_out = conv_w_out[j].astype(BF16)
            b_out = conv_b_out[j]
        else:
            nh = lru_w_a.shape[2]
            d_rnn = lru_w_sc.shape[2]
            hp = _round_up(d_rnn // nh, LANE)
            ph = functools.partial(_pad_heads, nh=nh, hp=hp)
            w_in_p = _pad_heads(lru_w_in[j].astype(BF16), 2 * nh, hp)
            b_in_p = _pad_heads(lru_b_in[j], 2 * nh, hp)
            yx = _inproj(x_parts, mods, 0, 1, w_in_p, b_in_p, mod_row, glu=False, tn=1024)

            def pad_sq(w):
                hb = w.shape[-1]
                return jnp.pad(w, [(0, 0), (0, 0), (0, hp - hb), (0, hp - hb)]).astype(BF16)

            rows = jnp.concatenate([
                ph(lru_w_sc[j]), ph(lru_b_sc[j])[None], ph(lru_b_a[j]), ph(lru_b_x[j]), ph(lru_lam[j])], axis=0)
            params = jnp.zeros((16, nh * hp), F32).at[:rows.shape[0]].set(rows)
            params = params.reshape(16, nh, hp).transpose(1, 0, 2)
            wa = pad_sq(lru_w_a[j])
            wx = pad_sq(lru_w_x[j])
            h0_p = jnp.zeros((batch, 2, nh * hp), F32)
            h0_s = ph(state_lru[:, j])
            mixed_p, fin_p = _lru_core(yx, 0, batch, seq, wa, wx, params, h0_p)
            mixed_s, _ = _lru_core(yx, n_p, dec_batch, dec_seq, wa, wx, params, h0_s)
            act_parts = [mixed_p, mixed_s]
            hb = d_rnn // nh
            new_states.append(fin_p.reshape(batch, 2, nh, hp)[..., :hb].reshape(batch, 2, d_rnn))
            w_out = jnp.pad(lru_w_out[j].reshape(nh, hb, d), [(0, 0), (0, hp - hb), (0, 0)]).reshape(nh * hp, d).astype(BF16)
            b_out = lru_b_out[j]
        z = _outproj(act_parts, w_out, b_out, x_parts, mods, 2, mod_row, alpha)
        tiles_p, tiles_s = n_p // ROW_TILE, n_s // ROW_TILE
        splits = [(0, tiles_p), (tiles_p, tiles_s)] if l == depth - 1 else [(0, tiles_p + tiles_s)]
        x_parts = _hier_moe_block(z, mods, ln_g[l, 0], ln_b[l, 0], ln_g[l, 1], ln_b[l, 1],
                                  moe_rg_w[l], moe_rg_b[l], moe_re_w[l], moe_re_b[l],
                                  moe_w1, moe_w3, moe_w2, l, mod_row, alpha, splits)

    y_prompt = x_parts[0].reshape(batch, seq, d)
    y_sample = x_parts[1].reshape(dec_batch, dec_seq, d)
    new_state_lru = jnp.stack(new_states, axis=1).astype(x_prompt.dtype)
    return (y_prompt, y_sample, new_state_lru)
```

```python
import functools
import math

import jax
import jax.numpy as jnp
from jax import lax
from jax.experimental import pallas as pl
from jax.experimental.pallas import tpu as pltpu

F32 = jnp.float32
BF16 = jnp.bfloat16

GRID_W = 64
TOP_K = 2
LRU_C = 8.0
LN_EPS = 1e-5
LANE = 128
SUBLANE = 8
VMEM_LIMIT = 56 * 1024 * 1024
MOE_TB = 256
ROW_TILE = 256
DMA_ISSUE_UNROLL = 8


def _sigmoid(x):
    return 1.0 / (1.0 + jnp.exp(-x))


def _pack_bf16_pair(a, b):
    hi = lax.bitcast_convert_type(a.astype(BF16).astype(F32), jnp.int32)
    lo = lax.bitcast_convert_type(b.astype(BF16).astype(F32), jnp.int32)
    return hi | lax.shift_right_logical(lo, 16)


def _unpack_hi(w):
    return lax.bitcast_convert_type(w & jnp.int32(-65536), F32)


def _unpack_lo(w):
    return lax.bitcast_convert_type(lax.shift_left(w, 16), F32)


def _round_up(x, m):
    return (x + m - 1) // m * m


def _cparams(n_axes):
    return pltpu.CompilerParams(
        dimension_semantics=("arbitrary",) * n_axes, vmem_limit_bytes=VMEM_LIMIT)


def _mod_row_fn(tile, n_prompt, dec_seq, ctx_row):
    def fn(i):
        t0 = i * tile
        return jnp.where(t0 < n_prompt, ctx_row, (t0 - n_prompt) // dec_seq)
    return fn


def _adaln_kernel(c_ref, w_ref, b_ref, o_ref):
    a = c_ref[...]
    a = (a * _sigmoid(a)).astype(BF16)
    o_ref[...] = jnp.dot(a, w_ref[...].astype(BF16), preferred_element_type=F32) + b_ref[...]


def _adaln(cond, mod_w, mod_b):
    depth, d, n6 = mod_w.shape
    r = cond.shape[0]
    tn = 512
    return pl.pallas_call(
        _adaln_kernel,
        out_shape=jax.ShapeDtypeStruct((depth, r, n6), F32),
        grid=(depth, n6 // tn),
        in_specs=[
            pl.BlockSpec((r, d), lambda l, j: (0, 0)),
            pl.BlockSpec((None, d, tn), lambda l, j: (l, 0, j)),
            pl.BlockSpec((None, 1, tn), lambda l, j: (l, 0, j)),
        ],
        out_specs=pl.BlockSpec((None, r, tn), lambda l, j: (l, 0, j)),
        compiler_params=_cparams(2),
        name="adaln",
    )(cond, mod_w, mod_b.reshape(depth, 1, n6))


def _row_sources(parts, tm, width_block, col_of, single_buffer=False):
    mode = dict(pipeline_mode=pl.Buffered(1)) if single_buffer else {}
    specs, starts = [], []
    t0 = 0
    for part in parts:
        nt = part.shape[0] // tm
        lo, hi = t0, t0 + nt

        def index(i, j, lo=lo, hi=hi):
            own = (i >= lo) & (i < hi)
            return (jnp.clip(i - lo, 0, hi - lo - 1), jnp.where(own, col_of(j), 0))

        specs.append(pl.BlockSpec((tm, width_block), index, **mode))
        starts.append(t0)
        t0 = hi
    return specs, starts


def _for_owner(starts, n_tiles, refs, fn):
    i = pl.program_id(0)
    bounds = list(starts) + [n_tiles]
    if len(refs) == 1:
        fn(refs[0])
        return
    for p, ref in enumerate(refs):
        pl.when((i >= bounds[p]) & (i < bounds[p + 1]))(functools.partial(fn, ref))


def _inproj_kernel(*refs, glu, n_src, starts, n_tiles):
    x_refs = refs[:n_src]
    sh_ref, sc_ref = refs[n_src:n_src + 2]
    rest = refs[n_src + 2:]
    if glu:
        wa_ref, wg_ref, ba_ref, bg_ref, o_ref, xs_ref = rest
    else:
        wa_ref, ba_ref, o_ref, xs_ref = rest

    def modulate(x_ref):
        xs_ref[...] = (x_ref[...] * (1.0 + sc_ref[...]) + sh_ref[...]).astype(BF16)

    @pl.when(pl.program_id(1) == 0)
    def _():
        _for_owner(starts, n_tiles, x_refs, modulate)

    xs = xs_ref[...]
    a = jnp.dot(xs, wa_ref[...], preferred_element_type=F32) + ba_ref[...]
    if glu:
        g = jnp.dot(xs, wg_ref[...], preferred_element_type=F32) + bg_ref[...]
        a = a * _sigmoid(g)
    o_ref[...] = a.astype(o_ref.dtype)


def _inproj(x_parts, mods, k_shift, k_scale, w, b, mod_row, *, glu, tm=512, tn=512):
    d = x_parts[0].shape[1]
    n = sum(p.shape[0] for p in x_parts)
    nw = w.shape[1]
    nout = nw // 2 if glu else nw
    tn = min(tn, nout)
    nj = nout // tn
    b2 = b.reshape(1, nw)
    row = lambda i, j: mod_row(i * (tm // ROW_TILE))
    x_specs, starts = _row_sources(x_parts, tm, d, lambda j: 0, single_buffer=len(x_parts) > 1)
    in_specs = x_specs + [
        pl.BlockSpec((None, None, 1, d), lambda i, j: (k_shift, row(i, j), 0, 0)),
        pl.BlockSpec((None, None, 1, d), lambda i, j: (k_scale, row(i, j), 0, 0)),
    ]
    args = list(x_parts) + [mods, mods]
    if glu:
        in_specs += [
            pl.BlockSpec((d, tn), lambda i, j: (0, j)),
            pl.BlockSpec((d, tn), lambda i, j: (0, j + nj)),
            pl.BlockSpec((1, tn), lambda i, j: (0, j)),
            pl.BlockSpec((1, tn), lambda i, j: (0, j + nj)),
        ]
        args += [w, w, b2, b2]
    else:
        in_specs += [
            pl.BlockSpec((d, tn), lambda i, j: (0, j)),
            pl.BlockSpec((1, tn), lambda i, j: (0, j)),
        ]
        args += [w, b2]
    return pl.pallas_call(
        functools.partial(_inproj_kernel, glu=glu, n_src=len(x_parts), starts=tuple(starts), n_tiles=n // tm),
        out_shape=jax.ShapeDtypeStruct((n, nout), F32),
        grid=(n // tm, nj),
        in_specs=in_specs,
        out_specs=pl.BlockSpec((tm, tn), lambda i, j: (i, j)),
        scratch_shapes=[pltpu.VMEM((tm, d), BF16)],
        compiler_params=_cparams(2),
        name="inproj_glu" if glu else "inproj",
    )(*args)


CONV_ROWS = 64
CONV_CC = 256
CONV_GAP = 16
CONV_CHUNK = 1024


def _dwconv_kernel(u_ref, w_ref, bdw_ref, g_ref, b_ref, o_ref, sh_ref, v_ref, *, seg):
    rows, c = u_ref.shape
    width = w_ref.shape[0]
    half = width // 2
    nseg = rows // seg
    stride = seg + CONV_GAP
    pad_rows = nseg * stride + CONV_GAP
    chunk_w = sh_ref.shape[2]
    zeros_gap = jnp.zeros((CONV_GAP, chunk_w), F32)

    def lane_chunk(ci, carry):
        l0 = pl.multiple_of(ci * chunk_w, chunk_w)
        for s in range(nseg):
            sh_ref[0, s * stride:s * stride + CONV_GAP, :] = zeros_gap
            sh_ref[0, s * stride + CONV_GAP:(s + 1) * stride, :] = u_ref[s * seg:(s + 1) * seg, pl.ds(l0, chunk_w)]
        sh_ref[0, nseg * stride:pad_rows, :] = zeros_gap
        for j in range(1, SUBLANE):
            sh_ref[j, 0:pad_rows - SUBLANE, :] = sh_ref[0, j:j + pad_rows - SUBLANE, :]

        for rb in range(rows // CONV_ROWS):
            r0 = rb * CONV_ROWS
            s, within = divmod(r0, seg)
            base = s * stride + CONV_GAP + within - half

            def sub(cc, carry2, base=base, r0=r0):
                c0 = pl.multiple_of(cc * CONV_CC, CONV_CC)
                g0 = pl.multiple_of(l0 + c0, CONV_CC)
                acc = jnp.zeros((CONV_ROWS, CONV_CC), F32) + bdw_ref[:, pl.ds(g0, CONV_CC)]
                for k in range(width):
                    j = (base + k) % SUBLANE
                    q = base + k - j
                    acc = acc + w_ref[k:k + 1, pl.ds(g0, CONV_CC)] * sh_ref[j, q:q + CONV_ROWS, pl.ds(c0, CONV_CC)]
                v_ref[r0:r0 + CONV_ROWS, pl.ds(g0, CONV_CC)] = acc
                return carry2

            lax.fori_loop(0, chunk_w // CONV_CC, sub, 0)
        return carry

    lax.fori_loop(0, c // chunk_w, lane_chunk, 0)

    ln_rows = 32

    def ln_block(r, carry):
        q0 = pl.multiple_of(r * ln_rows, ln_rows)
        v = v_ref[pl.ds(q0, ln_rows), :]
        mu = jnp.mean(v, axis=-1, keepdims=True)
        vc = v - mu
        var = jnp.mean(vc * vc, axis=-1, keepdims=True)
        y = vc * lax.rsqrt(var + LN_EPS) * g_ref[...] + b_ref[...]
        o_ref[pl.ds(q0, ln_rows), :] = (y * _sigmoid(y)).astype(o_ref.dtype)
        return carry

    lax.fori_loop(0, rows // ln_rows, ln_block, 0)


def _dwconv(u, row0, nrows, seg, w_dw, b_dw, g_n, b_n):
    c = u.shape[1]
    width = w_dw.shape[0]
    tile = ROW_TILE
    assert tile % seg == 0 and seg % CONV_ROWS == 0 and CONV_GAP >= width // 2
    off = row0 // tile
    nseg = tile // seg
    vec = lambda a: a.reshape(1, c)
    return pl.pallas_call(
        functools.partial(_dwconv_kernel, seg=seg),
        out_shape=jax.ShapeDtypeStruct((nrows, c), BF16),
        grid=(nrows // tile,),
        in_specs=[
            pl.BlockSpec((tile, c), lambda i: (i + off, 0)),
            pl.BlockSpec((width, c), lambda i: (0, 0)),
            pl.BlockSpec((1, c), lambda i: (0, 0)),
            pl.BlockSpec((1, c), lambda i: (0, 0)),
            pl.BlockSpec((1, c), lambda i: (0, 0)),
        ],
        out_specs=pl.BlockSpec((tile, c), lambda i: (i, 0)),
        scratch_shapes=[
            pltpu.VMEM((SUBLANE, nseg * (seg + CONV_GAP) + CONV_GAP, min(CONV_CHUNK, c)), F32),
            pltpu.VMEM((tile, c), F32),
        ],
        compiler_params=_cparams(1),
        name="dwconv_seg%d" % seg,
    )(u, w_dw, vec(b_dw), vec(g_n), vec(b_n))


def _outproj_kernel(*refs, alpha, n_a, a_starts, n_x, x_starts, n_tiles):
    a_refs = refs[:n_a]
    w_ref, bias_ref = refs[n_a:n_a + 2]
    x_refs = refs[n_a + 2:n_a + 2 + n_x]
    gate_ref, o_ref = refs[n_a + 2 + n_x:]

    def project(a_ref):
        o_ref[...] = gate_ref[...] * (jnp.dot(a_ref[...], w_ref[...], preferred_element_type=F32) + bias_ref[...])

    def add_residual(x_ref):
        o_ref[...] += alpha * x_ref[...]

    _for_owner(a_starts, n_tiles, a_refs, project)
    _for_owner(x_starts, n_tiles, x_refs, add_residual)


def _outproj(a_parts, w, bias, x_parts, mods, k_gate, mod_row, alpha, *, tm=512, tn=512):
    kdim = a_parts[0].shape[1]
    n = sum(p.shape[0] for p in a_parts)
    d = w.shape[1]
    tn = min(tn, d)
    row = lambda i: mod_row(i * (tm // ROW_TILE))
    a_specs, a_starts = _row_sources(a_parts, tm, kdim, lambda j: 0)
    x_specs, x_starts = _row_sources(x_parts, tm, tn, lambda j: j)
    return pl.pallas_call(
        functools.partial(_outproj_kernel, alpha=alpha, n_a=len(a_parts), a_starts=tuple(a_starts),
                          n_x=len(x_parts), x_starts=tuple(x_starts), n_tiles=n // tm),
        out_shape=jax.ShapeDtypeStruct((n, d), F32),
        grid=(n // tm, d // tn),
        in_specs=a_specs + [
            pl.BlockSpec((kdim, tn), lambda i, j: (0, j)),
            pl.BlockSpec((1, tn), lambda i, j: (0, j)),
        ] + x_specs + [
            pl.BlockSpec((None, None, 1, tn), lambda i, j: (k_gate, row(i), 0, j)),
        ],
        out_specs=pl.BlockSpec((tm, tn), lambda i, j: (i, j)),
        compiler_params=_cparams(2),
        name="outproj",
    )(*a_parts, w, bias.reshape(1, d), *x_parts, mods)


def _lru_kernel(y_ref, xb_ref, wa_ref, wx_ref, p_ref, h0_ref, mixed_ref, fin_ref,
                a_ref, b_ref, hf_ref, hb_ref):
    t, hp = xb_ref.shape
    xb = xb_ref[...]
    p = p_ref[...]
    row = lax.broadcasted_iota(jnp.int32, (t, hp), 0)

    def from_above(x, s):
        return jnp.where(row >= s, pltpu.roll(x, s, 0), 0.0)

    def from_below(x, s):
        return jnp.where(row < t - s, pltpu.roll(x, t - s, 0), 0.0)

    xc = (p[0:1] * from_above(xb, 2) + p[1:2] * from_above(xb, 1) + p[2:3] * xb
          + p[3:4] * from_below(xb, 1) + p[4:5])
    xcb = xc.astype(BF16)
    for d in range(2):
        r = _sigmoid(jnp.dot(xcb, wa_ref[d], preferred_element_type=F32) + p[5 + d:6 + d])
        i = _sigmoid(jnp.dot(xcb, wx_ref[d], preferred_element_type=F32) + p[7 + d:8 + d])
        z = -p[9 + d:10 + d]
        softplus = jnp.maximum(z, 0.0) + jnp.log(1.0 + jnp.exp(-jnp.abs(z)))
        a = jnp.exp((-LRU_C) * softplus * r)
        a_ref[d] = a
        q = 1.0 - a * a
        b_ref[d] = (q * lax.rsqrt(jnp.maximum(q, 1e-30))) * (i * xc)

    grow = lax.broadcasted_iota(jnp.int32, (SUBLANE, hp), 0)
    ngroups = t // SUBLANE

    def group(g, carry):
        cf, cb = carry
        f0 = pl.multiple_of(g * SUBLANE, SUBLANE)
        af = a_ref[0, pl.ds(f0, SUBLANE), :]
        bf = b_ref[0, pl.ds(f0, SUBLANE), :]
        r0 = pl.multiple_of((ngroups - 1 - g) * SUBLANE, SUBLANE)
        ab = a_ref[1, pl.ds(r0, SUBLANE), :]
        bb = b_ref[1, pl.ds(r0, SUBLANE), :]
        for s in (1, 2, 4):
            mf = grow >= s
            bf = jnp.where(mf, af * pltpu.roll(bf, s, 0) + bf, bf)
            af = jnp.where(mf, af * pltpu.roll(af, s, 0), af)
            mb = grow < SUBLANE - s
            bb = jnp.where(mb, ab * pltpu.roll(bb, SUBLANE - s, 0) + bb, bb)
            ab = jnp.where(mb, ab * pltpu.roll(ab, SUBLANE - s, 0), ab)
        hf = af * cf + bf
        hb = ab * cb + bb
        hf_ref[pl.ds(f0, SUBLANE), :] = hf
        hb_ref[pl.ds(r0, SUBLANE), :] = hb
        cf = jnp.broadcast_to(hf[SUBLANE - 1:SUBLANE, :], (SUBLANE, hp))
        cb = jnp.broadcast_to(hb[0:1, :], (SUBLANE, hp))
        return cf, cb

    h0 = h0_ref[...]
    cf0 = jnp.broadcast_to(h0[0:1, :], (SUBLANE, hp))
    cb0 = jnp.broadcast_to(h0[1:2, :], (SUBLANE, hp))
    cf, cb = lax.fori_loop(0, ngroups, group, (cf0, cb0))
    fin_ref[0:1, :] = cf[0:1, :]
    fin_ref[1:2, :] = cb[0:1, :]

    y = y_ref[...]
    gelu = 0.5 * y * (1.0 + jnp.tanh(math.sqrt(2.0 / math.pi) * (y + 0.044715 * (y * y * y))))
    mixed_ref[...] = ((hf_ref[...] + hb_ref[...]) * gelu).astype(mixed_ref.dtype)


def _lru_core(yx, row0, n_seq, t, wa, wx, params, h0):
    nh, _, hp = params.shape
    dp = nh * hp
    off = row0 // t
    return pl.pallas_call(
        _lru_kernel,
        out_shape=(jax.ShapeDtypeStruct((n_seq * t, dp), BF16),
                   jax.ShapeDtypeStruct((n_seq, 2, dp), F32)),
        grid=(nh, n_seq),
        in_specs=[
            pl.BlockSpec((t, hp), lambda h, s: (s + off, h)),
            pl.BlockSpec((t, hp), lambda h, s: (s + off, h + nh)),
            pl.BlockSpec((2, None, hp, hp), lambda h, s: (0, h, 0, 0)),
            pl.BlockSpec((2, None, hp, hp), lambda h, s: (0, h, 0, 0)),
            pl.BlockSpec((None, 16, hp), lambda h, s: (h, 0, 0)),
            pl.BlockSpec((None, 2, hp), lambda h, s: (s, 0, h)),
        ],
        out_specs=(pl.BlockSpec((t, hp), lambda h, s: (s, h)),
                   pl.BlockSpec((None, 2, hp), lambda h, s: (s, 0, h))),
        scratch_shapes=[
            pltpu.VMEM((2, t, hp), F32), pltpu.VMEM((2, t, hp), F32),
            pltpu.VMEM((t, hp), F32), pltpu.VMEM((t, hp), F32),
        ],
        compiler_params=_cparams(2),
        name="lru_core_t%d" % t,
    )(yx, yx, wa, wx, params, h0)


def _router_kernel(z_ref, g_ref, b_ref, sh_ref, sc_ref, wh_ref, wl_ref, br_ref,
                   x_ref, xm_ref, ri_ref, rw_ref, cnt_ref, *, n_groups, n_experts):
    z = z_ref[...]
    mu = jnp.mean(z, axis=-1, keepdims=True)
    zc = z - mu
    var = jnp.mean(zc * zc, axis=-1, keepdims=True)
    x = zc * lax.rsqrt(var + LN_EPS) * g_ref[...] + b_ref[...]
    x_ref[...] = x
    xm = x * (1.0 + sc_ref[...]) + sh_ref[...]
    half = xm.shape[1] // 2
    xm_ref[...] = _pack_bf16_pair(xm[:, :half], xm[:, half:])
    xh = xm.astype(BF16)
    xl = (xm - xh.astype(F32)).astype(BF16)
    wh = wh_ref[...]
    logits = (jnp.dot(xh, wh, preferred_element_type=F32)
              + jnp.dot(xh, wl_ref[...], preferred_element_type=F32)
              + jnp.dot(xl, wh, preferred_element_type=F32)) + br_ref[...]
    tt = logits.shape[0]
    lane = lax.broadcasted_iota(jnp.int32, (tt, LANE), 1).astype(F32)
    neg = jnp.float32(-jnp.inf)
    big = jnp.float32(LANE)
    epg = n_experts // n_groups
    is_grp = (lane >= n_experts) & (lane < n_experts + n_groups)
    gl = jnp.where(is_grp, logits, neg)
    gmax = jnp.max(gl, axis=-1, keepdims=True)
    gsum = jnp.sum(jnp.where(is_grp, jnp.exp(gl - gmax), 0.0), axis=-1, keepdims=True)
    p_grp = 1.0 / gsum
    grp = jnp.min(jnp.where(gl == gmax, lane, big), axis=-1, keepdims=True) - n_experts
    in_grp = (lane >= grp * epg) & (lane < (grp + 1.0) * epg)
    el = jnp.where(in_grp, logits, neg)
    t1 = jnp.max(el, axis=-1, keepdims=True)
    i1 = jnp.min(jnp.where(el == t1, lane, big), axis=-1, keepdims=True)
    el2 = jnp.where(lane == i1, neg, el)
    t2 = jnp.max(el2, axis=-1, keepdims=True)
    i2 = jnp.min(jnp.where(el2 == t2, lane, big), axis=-1, keepdims=True)
    e2 = jnp.exp(t2 - t1)
    w1 = p_grp / (1.0 + e2)
    w2 = p_grp * e2 / (1.0 + e2)

    @pl.when(pl.program_id(0) == 0)
    def _():
        cnt_ref[...] = jnp.zeros_like(cnt_ref)

    sel1 = lane == i1
    sel2 = lane == i2
    onehot = jnp.where(sel1 | sel2, 1.0, 0.0)
    tr = lax.broadcasted_iota(jnp.int32, (tt, tt), 0)
    tc = lax.broadcasted_iota(jnp.int32, (tt, tt), 1)
    before = jnp.where(tc < tr, 1.0, 0.0).astype(BF16)
    prior = jnp.dot(before, onehot.astype(BF16), preferred_element_type=F32) + cnt_ref[...]
    r1 = jnp.sum(jnp.where(sel1, prior, 0.0), axis=-1, keepdims=True)
    r2 = jnp.sum(jnp.where(sel2, prior, 0.0), axis=-1, keepdims=True)
    cnt_ref[...] += jnp.sum(onehot, axis=0, keepdims=True)
    ri = jnp.where(lane == 0, i1, jnp.where(lane == 1, i2, jnp.where(lane == 2, r1, jnp.where(lane == 3, r2, 0.0))))
    ri_ref[...] = ri.astype(jnp.int32)
    rw_ref[...] = jnp.where(lane == 0, w1, jnp.where(lane == 1, w2, 0.0))


def _router(z, g, b, mods, k_shift, k_scale, wr_hi, wr_lo, br, mod_row, n_groups, n_experts):
    n, d = z.shape
    tt = ROW_TILE
    vec = lambda v: v.reshape(1, d)
    return pl.pallas_call(
        functools.partial(_router_kernel, n_groups=n_groups, n_experts=n_experts),
        out_shape=(jax.ShapeDtypeStruct((n, d), F32),
                   jax.ShapeDtypeStruct((n, d // 2), jnp.int32),
                   jax.ShapeDtypeStruct((n, LANE), jnp.int32),
                   jax.ShapeDtypeStruct((n, LANE), F32),
                   jax.ShapeDtypeStruct((1, LANE), F32)),
        grid=(n // tt,),
        in_specs=[
            pl.BlockSpec((tt, d), lambda i: (i, 0)),
            pl.BlockSpec((1, d), lambda i: (0, 0)),
            pl.BlockSpec((1, d), lambda i: (0, 0)),
            pl.BlockSpec((None, None, 1, d), lambda i: (k_shift, mod_row(i), 0, 0)),
            pl.BlockSpec((None, None, 1, d), lambda i: (k_scale, mod_row(i), 0, 0)),
            pl.BlockSpec((d, LANE), lambda i: (0, 0)),
            pl.BlockSpec((d, LANE), lambda i: (0, 0)),
            pl.BlockSpec((1, LANE), lambda i: (0, 0)),
        ],
        out_specs=(pl.BlockSpec((tt, d), lambda i: (i, 0)),
                   pl.BlockSpec((tt, d // 2), lambda i: (i, 0)),
                   pl.BlockSpec((tt, LANE), lambda i: (i, 0)),
                   pl.BlockSpec((tt, LANE), lambda i: (i, 0)),
                   pl.BlockSpec((1, LANE), lambda i: (0, 0))),
        compiler_params=_cparams(1),
        name="moe_router",
    )(z, vec(g), vec(b), mods, mods, wr_hi, wr_lo, br)


def _for_rows_spread(n_rows, fn):
    n_tiles = n_rows // SUBLANE
    for q in range(SUBLANE):
        def body(t, carry, q=q):
            fn(t * SUBLANE + q)
            return carry
        lax.fori_loop(0, n_tiles, body, 0, unroll=DMA_ISSUE_UNROLL)


def _gather_kernel(nblk_ref, idx_ref, idx_next_ref, x_hbm, o_ref, buf_ref, sem):
    b = pl.program_id(0)
    nblk = nblk_ref[0]
    tb = idx_ref.shape[-1]
    slot = b % 2

    def issue(ids_ref, dst_slot):
        def copy_row(r):
            pltpu.make_async_copy(x_hbm.at[pl.ds(ids_ref[0, 0, r], 1)],
                                  buf_ref.at[dst_slot, pl.ds(r, 1)], sem.at[dst_slot]).start()
        _for_rows_spread(tb, copy_row)

    @pl.when(b == 0)
    def _():
        issue(idx_ref, 0)

    @pl.when(b + 1 < nblk)
    def _():
        issue(idx_next_ref, 1 - slot)

    @pl.when(b < nblk)
    def _():
        pltpu.make_async_copy(x_hbm.at[pl.ds(0, tb)], buf_ref.at[slot], sem.at[slot]).wait()
        w = buf_ref[slot]
        half = w.shape[1]
        o_ref[:, :half] = _unpack_hi(w).astype(o_ref.dtype)
        o_ref[:, half:] = _unpack_lo(w).astype(o_ref.dtype)

    @pl.when(b >= nblk)
    def _():
        o_ref[...] = jnp.zeros_like(o_ref)


def _gather_rows(x, idx, n_blocks):
    nb, _, tb = idx.shape
    d = 2 * x.shape[1]
    return pl.pallas_call(
        _gather_kernel,
        out_shape=jax.ShapeDtypeStruct((nb * tb, d), BF16),
        grid=(nb,),
        in_specs=[
            pl.BlockSpec(memory_space=pltpu.SMEM),
            pl.BlockSpec((1, 1, tb), lambda b: (b, 0, 0), memory_space=pltpu.SMEM),
            pl.BlockSpec((1, 1, tb), lambda b: (jnp.minimum(b + 1, nb - 1), 0, 0), memory_space=pltpu.SMEM),
            pl.BlockSpec(memory_space=pl.ANY),
        ],
        out_specs=pl.BlockSpec((tb, d), lambda b: (b, 0)),
        scratch_shapes=[pltpu.VMEM((2, tb, d // 2), x.dtype), pltpu.SemaphoreType.DMA((2,))],
        compiler_params=_cparams(1),
        name="moe_gather",
    )(n_blocks.reshape(1).astype(jnp.int32), idx, idx, x)


def _moe_up_kernel(sb_ref, ob_ref, oc_ref, snew_ref, sci_ref, nvalid_ref, ce_ref, cc_ref, nchg_ref,
                   xs_ref, w1_hbm, w3_hbm, h_ref, st1_ref, st3_ref, w1c_ref, w3c_ref, sem, *, layer):
    s = pl.program_id(0)
    fc = st1_ref.shape[1]

    def fetch(i):
        c0 = pl.multiple_of(cc_ref[i] * fc, fc)
        return (pltpu.make_async_copy(w1_hbm.at[layer, ce_ref[i], :, pl.ds(c0, fc)], st1_ref, sem.at[0]),
                pltpu.make_async_copy(w3_hbm.at[layer, ce_ref[i], :, pl.ds(c0, fc)], st3_ref, sem.at[1]))

    @pl.when(s == 0)
    def _():
        for cp in fetch(0):
            cp.start()

    @pl.when(snew_ref[s] == 1)
    def _():
        i = sci_ref[s]
        for cp in fetch(i):
            cp.wait()
        w1c_ref[...] = st1_ref[...].astype(BF16)
        w3c_ref[...] = st3_ref[...].astype(BF16)

        @pl.when(i + 1 < nchg_ref[0])
        def _():
            for cp in fetch(i + 1):
                cp.start()

    @pl.when(s < nvalid_ref[0])
    def _():
        xb = xs_ref[...]
        h1 = jnp.dot(xb, w1c_ref[...], preferred_element_type=F32)
        h3 = jnp.dot(xb, w3c_ref[...], preferred_element_type=F32)
        h_ref[...] = (h1 * _sigmoid(h1) * h3).astype(h_ref.dtype)

    @pl.when(s >= nvalid_ref[0])
    def _():
        h_ref[...] = jnp.zeros_like(h_ref)


def _moe_up(tables, xs, w1, w3, layer, fc):
    slots, d = xs.shape
    f = w1.shape[3]
    t = tables
    prefetch = (t["sb"], t["ob"], t["oc"], t["snew"], t["sci"], t["nvalid"], t["ce"], t["cc"], t["nchg"])
    nsteps = t["sb"].shape[0]
    return pl.pallas_call(
        functools.partial(_moe_up_kernel, layer=layer),
        out_shape=jax.ShapeDtypeStruct((slots, f), BF16),
        grid_spec=pltpu.PrefetchScalarGridSpec(
            num_scalar_prefetch=len(prefetch),
            grid=(nsteps,),
            in_specs=[
                pl.BlockSpec((MOE_TB, d), lambda s, sb, *_: (sb[s], 0)),
                pl.BlockSpec(memory_space=pl.ANY),
                pl.BlockSpec(memory_space=pl.ANY),
            ],
            out_specs=pl.BlockSpec((MOE_TB, fc), lambda s, sb, ob, oc, *_: (ob[s], oc[s])),
            scratch_shapes=[pltpu.VMEM((d, fc), F32), pltpu.VMEM((d, fc), F32),
                            pltpu.VMEM((d, fc), BF16), pltpu.VMEM((d, fc), BF16),
                            pltpu.SemaphoreType.DMA((2,))],
        ),
        compiler_params=_cparams(1),
        name="moe_up",
    )(*prefetch, xs, w1, w3)


def _moe_down_kernel(sb_ref, ob_ref, snew_ref, sci_ref, nvalid_ref, ce_ref, nchg_ref,
                     h_ref, w2_hbm, y_ref, st_ref, w2c_ref, sem, *, layer):
    s = pl.program_id(0)

    def fetch(i):
        return pltpu.make_async_copy(w2_hbm.at[layer, ce_ref[i]], st_ref, sem.at[0])

    @pl.when(s == 0)
    def _():
        fetch(0).start()

    @pl.when(snew_ref[s] == 1)
    def _():
        i = sci_ref[s]
        fetch(i).wait()
        w2c_ref[...] = st_ref[...].astype(BF16)

        @pl.when(i + 1 < nchg_ref[0])
        def _():
            fetch(i + 1).start()

    @pl.when(s < nvalid_ref[0])
    def _():
        y = jnp.dot(h_ref[...], w2c_ref[...], preferred_element_type=F32)
        half = y.shape[1] // 2
        y_ref[...] = _pack_bf16_pair(y[:, :half], y[:, half:])

    @pl.when(s >= nvalid_ref[0])
    def _():
        y_ref[...] = jnp.zeros_like(y_ref)


def _moe_down(tables, h, w2, layer):
    slots, f = h.shape
    d = w2.shape[3]
    t = tables
    prefetch = (t["sb"], t["ob"], t["snew"], t["sci"], t["nvalid"], t["ce"], t["nchg"])
    nsteps = t["sb"].shape[0]
    return pl.pallas_call(
        functools.partial(_moe_down_kernel, layer=layer),
        out_shape=jax.ShapeDtypeStruct((slots, d // 2), jnp.int32),
        grid_spec=pltpu.PrefetchScalarGridSpec(
            num_scalar_prefetch=len(prefetch),
            grid=(nsteps,),
            in_specs=[
                pl.BlockSpec((MOE_TB, f), lambda s, sb, *_: (sb[s], 0)),
                pl.BlockSpec(memory_space=pl.ANY),
            ],
            out_specs=pl.BlockSpec((MOE_TB, d // 2), lambda s, sb, ob, *_: (ob[s], 0)),
            scratch_shapes=[pltpu.VMEM((f, d), F32), pltpu.VMEM((f, d), BF16), pltpu.SemaphoreType.DMA((1,))],
        ),
        compiler_params=_cparams(1),
        name="moe_down",
    )(*prefetch, h, w2)


def _combine_kernel(dest_ref, dest_next_ref, y_hbm, x_ref, rw_ref, gate_ref, g_ref, b_ref, o_ref, ybuf_ref, sem, *, alpha):
    i = pl.program_id(0)
    nt = pl.num_programs(0)
    tt = x_ref.shape[0]
    slot = i % 2

    def issue(ids_ref, dst_slot):
        def copy_rows(r):
            for k in range(TOP_K):
                pltpu.make_async_copy(y_hbm.at[pl.ds(ids_ref[0, 0, k * tt + r], 1)],
                                      ybuf_ref.at[dst_slot, k, pl.ds(r, 1)], sem.at[dst_slot]).start()
        _for_rows_spread(tt, copy_rows)

    @pl.when(i == 0)
    def _():
        issue(dest_ref, 0)

    @pl.when(i + 1 < nt)
    def _():
        issue(dest_next_ref, 1 - slot)

    for k in range(TOP_K):
        pltpu.make_async_copy(y_hbm.at[pl.ds(0, tt)], ybuf_ref.at[slot, k], sem.at[slot]).wait()

    rw = rw_ref[...]
    y0 = ybuf_ref[slot, 0]
    y1 = ybuf_ref[slot, 1]
    f = jnp.concatenate([rw[:, 0:1] * _unpack_hi(y0) + rw[:, 1:2] * _unpack_hi(y1),
                         rw[:, 0:1] * _unpack_lo(y0) + rw[:, 1:2] * _unpack_lo(y1)], axis=1)
    z = alpha * x_ref[...] + gate_ref[...] * f
    mu = jnp.mean(z, axis=-1, keepdims=True)
    zc = z - mu
    var = jnp.mean(zc * zc, axis=-1, keepdims=True)
    o_ref[...] = zc * lax.rsqrt(var + LN_EPS) * g_ref[...] + b_ref[...]


def _combine(dest, y, x, rw, mods, k_gate, g, b, mod_row, alpha, tile0, ntiles):
    d = x.shape[1]
    tt = ROW_TILE
    vec = lambda v: v.reshape(1, d)
    return pl.pallas_call(
        functools.partial(_combine_kernel, alpha=alpha),
        out_shape=jax.ShapeDtypeStruct((ntiles * tt, d), F32),
        grid=(ntiles,),
        in_specs=[
            pl.BlockSpec((1, 1, TOP_K * tt), lambda i: (i + tile0, 0, 0), memory_space=pltpu.SMEM),
            pl.BlockSpec((1, 1, TOP_K * tt), lambda i: (jnp.minimum(i + 1, ntiles - 1) + tile0, 0, 0),
                         memory_space=pltpu.SMEM),
            pl.BlockSpec(memory_space=pl.ANY),
            pl.BlockSpec((tt, d), lambda i: (i + tile0, 0)),
            pl.BlockSpec((tt, LANE), lambda i: (i + tile0, 0)),
            pl.BlockSpec((None, None, 1, d), lambda i: (k_gate, mod_row(i + tile0), 0, 0)),
            pl.BlockSpec((1, d), lambda i: (0, 0)),
            pl.BlockSpec((1, d), lambda i: (0, 0)),
        ],
        out_specs=pl.BlockSpec((tt, d), lambda i: (i, 0)),
        scratch_shapes=[pltpu.VMEM((2, TOP_K, tt, d // 2), jnp.int32), pltpu.SemaphoreType.DMA((2,))],
        compiler_params=_cparams(1),
        name="moe_combine_ln",
    )(dest, dest, y, x, rw, mods, vec(g), vec(b))


def _dispatch_tables(flat_e, flat_rank, counts, n_chunks_up, n_chunks_down):
    nk = flat_e.shape[0]
    n_experts = counts.shape[0]
    tb = MOE_TB
    nb = nk // tb + n_experts
    eye = (flat_e[:, None] == jnp.arange(n_experts, dtype=jnp.int32)[None, :])
    blocks_e = (counts + tb - 1) // tb
    blk_end = jnp.cumsum(blocks_e)
    blk_start = blk_end - blocks_e
    dest = jnp.sum(jnp.where(eye, blk_start[None, :] * tb, 0), axis=1) + flat_rank
    slot_tok = jnp.zeros((nb * tb,), jnp.int32).at[dest].set(jnp.arange(nk, dtype=jnp.int32) // TOP_K)
    n_blocks = blk_end[-1]

    def steps(n_chunks):
        ns = nb * n_chunks
        per_e = blocks_e * n_chunks
        e_end = jnp.cumsum(per_e)
        e_start = e_end - per_e
        s = jnp.arange(ns, dtype=jnp.int32)
        nvalid = e_end[-1]
        sv = jnp.minimum(s, nvalid - 1)
        e = jnp.minimum(jnp.sum((sv[:, None] >= e_end[None, :]).astype(jnp.int32), axis=1), n_experts - 1)
        pick = e[:, None] == jnp.arange(n_experts, dtype=jnp.int32)[None, :]
        local = sv - jnp.sum(jnp.where(pick, e_start[None, :], 0), axis=1)
        be = jnp.maximum(jnp.sum(jnp.where(pick, blocks_e[None, :], 0), axis=1), 1)
        chunk = (local // be).astype(jnp.int32)
        blk = (jnp.sum(jnp.where(pick, blk_start[None, :], 0), axis=1) + local % be).astype(jnp.int32)
        key = e * n_chunks + chunk
        new = jnp.concatenate([jnp.ones((1,), jnp.int32), (key[1:] != key[:-1]).astype(jnp.int32)])
        extra = s - nvalid
        out_blk = jnp.where(s < nvalid, blk, n_blocks + extra // n_chunks).astype(jnp.int32)
        out_chunk = jnp.where(s < nvalid, chunk, extra % n_chunks).astype(jnp.int32)
        n_pairs = n_experts * n_chunks
        pair = jnp.arange(n_pairs, dtype=jnp.int32)
        live = jnp.repeat(blocks_e > 0, n_chunks)
        pos = jnp.where(live, jnp.cumsum(live.astype(jnp.int32)) - 1, n_pairs)
        ce = jnp.zeros((n_pairs,), jnp.int32).at[pos].set(pair // n_chunks, mode='drop')
        cc = jnp.zeros((n_pairs,), jnp.int32).at[pos].set(pair % n_chunks, mode='drop')
        as1 = lambda v: v.reshape(1).astype(jnp.int32)
        return dict(sb=blk, ob=out_blk, oc=out_chunk, snew=new, sci=(jnp.cumsum(new) - 1).astype(jnp.int32),
                    nvalid=as1(nvalid), ce=ce, cc=cc, nchg=as1(jnp.sum(live)))

    return dest, slot_tok.reshape(nb, 1, tb), steps(n_chunks_up), steps(n_chunks_down), n_blocks


def _hier_moe_block(z, mods, ln1_g, ln1_b, ln2_g, ln2_b, rg_w, rg_b, re_w, re_b, w1, w3, w2, layer, mod_row, alpha,
                    out_splits):
    n, d = z.shape
    n_groups = rg_w.shape[1]
    n_experts = re_w.shape[1]
    f = w1.shape[3]
    wr = jnp.zeros((d, LANE), F32).at[:, :n_experts].set(re_w).at[:, n_experts:n_experts + n_groups].set(rg_w)
    br = jnp.zeros((1, LANE), F32).at[0, :n_experts].set(re_b).at[0, n_experts:n_experts + n_groups].set(rg_b)
    wr_hi = wr.astype(BF16)
    wr_lo = (wr - wr_hi.astype(F32)).astype(BF16)
    x, xm, ri, rw, cnt = _router(z, ln1_g, ln1_b, mods, 3, 4, wr_hi, wr_lo, br, mod_row, n_groups, n_experts)
    flat_e = ri[:, :TOP_K].reshape(-1)
    flat_rank = ri[:, TOP_K:2 * TOP_K].reshape(-1)
    counts = cnt[0, :n_experts].astype(jnp.int32)
    fc = min(512, f)
    dest, slot_tok, up_tables, down_tables, n_blocks = _dispatch_tables(flat_e, flat_rank, counts, f // fc, 1)
    xs = _gather_rows(xm, slot_tok, n_blocks)
    h = _moe_up(up_tables, xs, w1, w3, layer, fc)
    y = _moe_down(down_tables, h, w2, layer)
    tt = ROW_TILE
    dest_t = dest.reshape(n // tt, tt, TOP_K).transpose(0, 2, 1).reshape(n // tt, 1, TOP_K * tt)
    return [_combine(dest_t, y, x, rw, mods, 5, ln2_g, ln2_b, mod_row, alpha, t0, nt) for t0, nt in out_splits]


def _pad_heads(v, nh, hp):
    lead = v.shape[:-1]
    hb = v.shape[-1] // nh
    v = v.reshape(lead + (nh, hb))
    v = jnp.pad(v, [(0, 0)] * len(lead) + [(0, 0), (0, hp - hb)])
    return v.reshape(lead + (nh * hp,))


def kernel(x_prompt, x_sample, state_lru, c, c_ctx, mod_w, mod_b, ln_g, ln_b, conv_w_in, conv_b_in, conv_w_dw, conv_b_dw, conv_ln_g, conv_ln_b, conv_w_out, conv_b_out, lru_w_in, lru_b_in, lru_w_sc, lru_b_sc, lru_w_a, lru_b_a, lru_w_x, lru_b_x, lru_lam, lru_w_out, lru_b_out, moe_rg_w, moe_rg_b, moe_re_w, moe_re_b, moe_w1, moe_w3, moe_w2):
    batch, seq, d = x_prompt.shape
    dec_batch, dec_seq, _ = x_sample.shape
    depth = mod_w.shape[0]
    n_p = batch * seq
    n_s = dec_batch * dec_seq
    alpha = (2 * depth) ** 0.25
    assert n_p % 512 == 0 and dec_seq % 512 == 0 and seq % ROW_TILE == 0 and ROW_TILE % GRID_W == 0

    x_parts = [x_prompt.reshape(n_p, d), x_sample.reshape(n_s, d)]

    n_rows = _round_up(dec_batch + 1, SUBLANE)
    cond = jnp.zeros((n_rows, d), F32).at[:dec_batch].set(c).at[dec_batch].set(c_ctx)
    m_all = _adaln(cond, mod_w, mod_b)
    m_all = m_all.reshape(depth, n_rows, 6, 1, d).transpose(0, 2, 1, 3, 4)
    mod_row = _mod_row_fn(ROW_TILE, n_p, dec_seq, dec_batch)

    new_states = []
    for l in range(depth):
        j = l // 2
        mods = m_all[l]
        if l % 2 == 0:
            u = _inproj(x_parts, mods, 0, 1, conv_w_in[j].astype(BF16), conv_b_in[j], mod_row, glu=True)
            dw = (conv_w_dw[j], conv_b_dw[j], conv_ln_g[j], conv_ln_b[j])
            act_parts = [_dwconv(u, 0, n_p, seq, *dw), _dwconv(u, n_p, n_s, GRID_W, *dw)]
            w_out = conv_w_out[j].astype(BF16)
            b_out = conv_b_out[j]
        else:
            nh = lru_w_a.shape[2]
            d_rnn = lru_w_sc.shape[2]
            hp = _round_up(d_rnn // nh, LANE)
            ph = functools.partial(_pad_heads, nh=nh, hp=hp)
            w_in_p = _pad_heads(lru_w_in[j].astype(BF16), 2 * nh, hp)
            b_in_p = _pad_heads(lru_b_in[j], 2 * nh, hp)
            yx = _inproj(x_parts, mods, 0, 1, w_in_p, b_in_p, mod_row, glu=False, tn=1024)

            def pad_sq(w):
                hb = w.shape[-1]
                return jnp.pad(w, [(0, 0), (0, 0), (0, hp - hb), (0, hp - hb)]).astype(BF16)

            rows = jnp.concatenate([
                ph(lru_w_sc[j]), ph(lru_b_sc[j])[None], ph(lru_b_a[j]), ph(lru_b_x[j]), ph(lru_lam[j])], axis=0)
            params = jnp.zeros((16, nh * hp), F32).at[:rows.shape[0]].set(rows)
            params = params.reshape(16, nh, hp).transpose(1, 0, 2)
            wa = pad_sq(lru_w_a[j])
            wx = pad_sq(lru_w_x[j])
            h0_p = jnp.zeros((batch, 2, nh * hp), F32)
            h0_s = ph(state_lru[:, j])
            mixed_p, fin_p = _lru_core(yx, 0, batch, seq, wa, wx, params, h0_p)
            mixed_s, _ = _lru_core(yx, n_p, dec_batch, dec_seq, wa, wx, params, h0_s)
            act_parts = [mixed_p, mixed_s]
            hb = d_rnn // nh
            new_states.append(fin_p.reshape(batch, 2, nh, hp)[..., :hb].reshape(batch, 2, d_rnn))
            w_out = jnp.pad(lru_w_out[j].reshape(nh, hb, d), [(0, 0), (0, hp - hb), (0, 0)]).reshape(nh * hp, d).astype(BF16)
            b_out = lru_b_out[j]
        z = _outproj(act_parts, w_out, b_out, x_parts, mods, 2, mod_row, alpha)
        tiles_p, tiles_s = n_p // ROW_TILE, n_s // ROW_TILE
        splits = [(0, tiles_p), (tiles_p, tiles_s)] if l == depth - 1 else [(0, tiles_p + tiles_s)]
        x_parts = _hier_moe_block(z, mods, ln_g[l, 0], ln_b[l, 0], ln_g[l, 1], ln_b[l, 1],
                                  moe_rg_w[l], moe_rg_b[l], moe_re_w[l], moe_re_b[l],
                                  moe_w1, moe_w3, moe_w2, l, mod_row, alpha, splits)

    y_prompt = x_parts[0].reshape(batch, seq, d)
    y_sample = x_parts[1].reshape(dec_batch, dec_seq, d)
    new_state_lru = jnp.stack(new_states, axis=1).astype(x_prompt.dtype)
    return (y_prompt, y_sample, new_state_lru)
```

```python
import functools
import math

import jax
import jax.numpy as jnp
from jax import lax
from jax.experimental import pallas as pl
from jax.experimental.pallas import tpu as pltpu

F32 = jnp.float32
BF16 = jnp.bfloat16

GRID_W = 64
TOP_K = 2
LRU_C = 8.0
LN_EPS = 1e-5
LANE = 128
SUBLANE = 8
VMEM_LIMIT = 56 * 1024 * 1024
MOE_TB = 256
ROW_TILE = 256
DMA_ISSUE_UNROLL = 8


def _sigmoid(x):
    return 1.0 / (1.0 + jnp.exp(-x))


def _pack_bf16_pair(a, b):
    hi = lax.bitcast_convert_type(a.astype(BF16).astype(F32), jnp.int32)
    lo = lax.bitcast_convert_type(b.astype(BF16).astype(F32), jnp.int32)
    return hi | lax.shift_right_logical(lo, 16)


def _unpack_hi(w):
    return lax.bitcast_convert_type(w & jnp.int32(-65536), F32)


def _unpack_lo(w):
    return lax.bitcast_convert_type(lax.shift_left(w, 16), F32)


def _round_up(x, m):
    return (x + m - 1) // m * m


def _cparams(n_axes):
    return pltpu.CompilerParams(
        dimension_semantics=("arbitrary",) * n_axes, vmem_limit_bytes=VMEM_LIMIT)


def _mod_row_fn(tile, n_prompt, dec_seq, ctx_row):
    def fn(i):
        t0 = i * tile
        return jnp.where(t0 < n_prompt, ctx_row, (t0 - n_prompt) // dec_seq)
    return fn


def _adaln_kernel(c_ref, w_ref, b_ref, o_ref):
    a = c_ref[...]
    a = (a * _sigmoid(a)).astype(BF16)
    o_ref[...] = jnp.dot(a, w_ref[...].astype(BF16), preferred_element_type=F32) + b_ref[...]


def _adaln(cond, mod_w, mod_b):
    depth, d, n6 = mod_w.shape
    r = cond.shape[0]
    tn = 512
    return pl.pallas_call(
        _adaln_kernel,
        out_shape=jax.ShapeDtypeStruct((depth, r, n6), F32),
        grid=(depth, n6 // tn),
        in_specs=[
            pl.BlockSpec((r, d), lambda l, j: (0, 0)),
            pl.BlockSpec((None, d, tn), lambda l, j: (l, 0, j)),
            pl.BlockSpec((None, 1, tn), lambda l, j: (l, 0, j)),
        ],
        out_specs=pl.BlockSpec((None, r, tn), lambda l, j: (l, 0, j)),
        compiler_params=_cparams(2),
        name="adaln",
    )(cond, mod_w, mod_b.reshape(depth, 1, n6))


def _row_sources(parts, tm, width_block, col_of, single_buffer=False):
    mode = dict(pipeline_mode=pl.Buffered(1)) if single_buffer else {}
    specs, starts = [], []
    t0 = 0
    for part in parts:
        nt = part.shape[0] // tm
        lo, hi = t0, t0 + nt

        def index(i, j, lo=lo, hi=hi):
            own = (i >= lo) & (i < hi)
            return (jnp.clip(i - lo, 0, hi - lo - 1), jnp.where(own, col_of(j), 0))

        specs.append(pl.BlockSpec((tm, width_block), index, **mode))
        starts.append(t0)
        t0 = hi
    return specs, starts


def _for_owner(starts, n_tiles, refs, fn):
    i = pl.program_id(0)
    bounds = list(starts) + [n_tiles]
    if len(refs) == 1:
        fn(refs[0])
        return
    for p, ref in enumerate(refs):
        pl.when((i >= bounds[p]) & (i < bounds[p + 1]))(functools.partial(fn, ref))


def _inproj_kernel(*refs, glu, n_src, starts, n_tiles):
    x_refs = refs[:n_src]
    sh_ref, sc_ref = refs[n_src:n_src + 2]
    rest = refs[n_src + 2:]
    if glu:
        wa_ref, wg_ref, ba_ref, bg_ref, o_ref, xs_ref = rest
    else:
        wa_ref, ba_ref, o_ref, xs_ref = rest

    def modulate(x_ref):
        xs_ref[...] = (x_ref[...] * (1.0 + sc_ref[...]) + sh_ref[...]).astype(BF16)

    @pl.when(pl.program_id(1) == 0)
    def _():
        _for_owner(starts, n_tiles, x_refs, modulate)

    xs = xs_ref[...]
    a = jnp.dot(xs, wa_ref[...], preferred_element_type=F32) + ba_ref[...]
    if glu:
        g = jnp.dot(xs, wg_ref[...], preferred_element_type=F32) + bg_ref[...]
        a = a * _sigmoid(g)
    o_ref[...] = a.astype(o_ref.dtype)


def _inproj(x_parts, mods, k_shift, k_scale, w, b, mod_row, *, glu, tm=512, tn=512):
    d = x_parts[0].shape[1]
    n = sum(p.shape[0] for p in x_parts)
    nw = w.shape[1]
    nout = nw // 2 if glu else nw
    tn = min(tn, nout)
    nj = nout // tn
    b2 = b.reshape(1, nw)
    row = lambda i, j: mod_row(i * (tm // ROW_TILE))
    x_specs, starts = _row_sources(x_parts, tm, d, lambda j: 0, single_buffer=len(x_parts) > 1)
    in_specs = x_specs + [
        pl.BlockSpec((None, None, 1, d), lambda i, j: (k_shift, row(i, j), 0, 0)),
        pl.BlockSpec((None, None, 1, d), lambda i, j: (k_scale, row(i, j), 0, 0)),
    ]
    args = list(x_parts) + [mods, mods]
    if glu:
        in_specs += [
            pl.BlockSpec((d, tn), lambda i, j: (0, j)),
            pl.BlockSpec((d, tn), lambda i, j: (0, j + nj)),
            pl.BlockSpec((1, tn), lambda i, j: (0, j)),
            pl.BlockSpec((1, tn), lambda i, j: (0, j + nj)),
        ]
        args += [w, w, b2, b2]
    else:
        in_specs += [
            pl.BlockSpec((d, tn), lambda i, j: (0, j)),
            pl.BlockSpec((1, tn), lambda i, j: (0, j)),
        ]
        args += [w, b2]
    return pl.pallas_call(
        functools.partial(_inproj_kernel, glu=glu, n_src=len(x_parts), starts=tuple(starts), n_tiles=n // tm),
        out_shape=jax.ShapeDtypeStruct((n, nout), F32),
        grid=(n // tm, nj),
        in_specs=in_specs,
        out_specs=pl.BlockSpec((tm, tn), lambda i, j: (i, j)),
        scratch_shapes=[pltpu.VMEM((tm, d), BF16)],
        compiler_params=_cparams(2),
        name="inproj_glu" if glu else "inproj",
    )(*args)


CONV_ROWS = 64
CONV_CC = 256
CONV_GAP = 16
CONV_CHUNK = 1024


def _dwconv_kernel(u_ref, w_ref, bdw_ref, g_ref, b_ref, o_ref, sh_ref, v_ref, *, seg):
    rows, c = u_ref.shape
    width = w_ref.shape[0]
    half = width // 2
    nseg = rows // seg
    stride = seg + CONV_GAP
    pad_rows = nseg * stride + CONV_GAP
    chunk_w = sh_ref.shape[2]
    zeros_gap = jnp.zeros((CONV_GAP, chunk_w), F32)

    def lane_chunk(ci, carry):
        l0 = pl.multiple_of(ci * chunk_w, chunk_w)
        for s in range(nseg):
            sh_ref[0, s * stride:s * stride + CONV_GAP, :] = zeros_gap
            sh_ref[0, s * stride + CONV_GAP:(s + 1) * stride, :] = u_ref[s * seg:(s + 1) * seg, pl.ds(l0, chunk_w)]
        sh_ref[0, nseg * stride:pad_rows, :] = zeros_gap
        for j in range(1, SUBLANE):
            sh_ref[j, 0:pad_rows - SUBLANE, :] = sh_ref[0, j:j + pad_rows - SUBLANE, :]

        for rb in range(rows // CONV_ROWS):
            r0 = rb * CONV_ROWS
            s, within = divmod(r0, seg)
            base = s * stride + CONV_GAP + within - half

            def sub(cc, carry2, base=base, r0=r0):
                c0 = pl.multiple_of(cc * CONV_CC, CONV_CC)
                g0 = pl.multiple_of(l0 + c0, CONV_CC)
                acc = jnp.zeros((CONV_ROWS, CONV_CC), F32) + bdw_ref[:, pl.ds(g0, CONV_CC)]
                for k in range(width):
                    j = (base + k) % SUBLANE
                    q = base + k - j
                    acc = acc + w_ref[k:k + 1, pl.ds(g0, CONV_CC)] * sh_ref[j, q:q + CONV_ROWS, pl.ds(c0, CONV_CC)]
                v_ref[r0:r0 + CONV_ROWS, pl.ds(g0, CONV_CC)] = acc
                return carry2

            lax.fori_loop(0, chunk_w // CONV_CC, sub, 0)
        return carry

    lax.fori_loop(0, c // chunk_w, lane_chunk, 0)

    ln_rows = 32

    def ln_block(r, carry):
        q0 = pl.multiple_of(r * ln_rows, ln_rows)
        v = v_ref[pl.ds(q0, ln_rows), :]
        mu = jnp.mean(v, axis=-1, keepdims=True)
        vc = v - mu
        var = jnp.mean(vc * vc, axis=-1, keepdims=True)
        y = vc * lax.rsqrt(var + LN_EPS) * g_ref[...] + b_ref[...]
        o_ref[pl.ds(q0, ln_rows), :] = (y * _sigmoid(y)).astype(o_ref.dtype)
        return carry

    lax.fori_loop(0, rows // ln_rows, ln_block, 0)


def _dwconv(u, row0, nrows, seg, w_dw, b_dw, g_n, b_n):
    c = u.shape[1]
    width = w_dw.shape[0]
    tile = ROW_TILE
    assert tile % seg == 0 and seg % CONV_ROWS == 0 and CONV_GAP >= width // 2
    off = row0 // tile
    nseg = tile // seg
    vec = lambda a: a.reshape(1, c)
    return pl.pallas_call(
        functools.partial(_dwconv_kernel, seg=seg),
        out_shape=jax.ShapeDtypeStruct((nrows, c), BF16),
        grid=(nrows // tile,),
        in_specs=[
            pl.BlockSpec((tile, c), lambda i: (i + off, 0)),
            pl.BlockSpec((width, c), lambda i: (0, 0)),
            pl.BlockSpec((1, c), lambda i: (0, 0)),
            pl.BlockSpec((1, c), lambda i: (0, 0)),
            pl.BlockSpec((1, c), lambda i: (0, 0)),
        ],
        out_specs=pl.BlockSpec((tile, c), lambda i: (i, 0)),
        scratch_shapes=[
            pltpu.VMEM((SUBLANE, nseg * (seg + CONV_GAP) + CONV_GAP, min(CONV_CHUNK, c)), F32),
            pltpu.VMEM((tile, c), F32),
        ],
        compiler_params=_cparams(1),
        name="dwconv_seg%d" % seg,
    )(u, w_dw, vec(b_dw), vec(g_n), vec(b_n))


def _outproj_kernel(*refs, alpha, n_a, a_starts, n_x, x_starts, n_tiles):
    a_refs = refs[:n_a]
    w_ref, bias_ref = refs[n_a:n_a + 2]
    x_refs = refs[n_a + 2:n_a + 2 + n_x]
    gate_ref, o_ref = refs[n_a + 2 + n_x:]

    def project(a_ref):
        o_ref[...] = gate_ref[...] * (jnp.dot(a_ref[...], w_ref[...], preferred_element_type=F32) + bias_ref[...])

    def add_residual(x_ref):
        o_ref[...] += alpha * x_ref[...]

    _for_owner(a_starts, n_tiles, a_refs, project)
    _for_owner(x_starts, n_tiles, x_refs, add_residual)


def _outproj(a_parts, w, bias, x_parts, mods, k_gate, mod_row, alpha, *, tm=512, tn=512):
    kdim = a_parts[0].shape[1]
    n = sum(p.shape[0] for p in a_parts)
    d = w.shape[1]
    if kdim <= 4096 and d % (2 * tn) == 0:
        tn = 2 * tn
    tn = min(tn, d)
    row = lambda i: mod_row(i * (tm // ROW_TILE))
    a_specs, a_starts = _row_sources(a_parts, tm, kdim, lambda j: 0)
    x_specs, x_starts = _row_sources(x_parts, tm, tn, lambda j: j)
    return pl.pallas_call(
        functools.partial(_outproj_kernel, alpha=alpha, n_a=len(a_parts), a_starts=tuple(a_starts),
                          n_x=len(x_parts), x_starts=tuple(x_starts), n_tiles=n // tm),
        out_shape=jax.ShapeDtypeStruct((n, d), F32),
        grid=(n // tm, d // tn),
        in_specs=a_specs + [
            pl.BlockSpec((kdim, tn), lambda i, j: (0, j)),
            pl.BlockSpec((1, tn), lambda i, j: (0, j)),
        ] + x_specs + [
            pl.BlockSpec((None, None, 1, tn), lambda i, j: (k_gate, row(i), 0, j)),
        ],
        out_specs=pl.BlockSpec((tm, tn), lambda i, j: (i, j)),
        compiler_params=_cparams(2),
        name="outproj",
    )(*a_parts, w, bias.reshape(1, d), *x_parts, mods)


def _lru_kernel(y_ref, xb_ref, wa_ref, wx_ref, p_ref, h0_ref, mixed_ref, fin_ref,
                a_ref, b_ref, hf_ref, hb_ref):
    t, hp = xb_ref.shape
    xb = xb_ref[...]
    p = p_ref[...]
    row = lax.broadcasted_iota(jnp.int32, (t, hp), 0)

    def from_above(x, s):
        return jnp.where(row >= s, pltpu.roll(x, s, 0), 0.0)

    def from_below(x, s):
        return jnp.where(row < t - s, pltpu.roll(x, t - s, 0), 0.0)

    xc = (p[0:1] * from_above(xb, 2) + p[1:2] * from_above(xb, 1) + p[2:3] * xb
          + p[3:4] * from_below(xb, 1) + p[4:5])
    xcb = xc.astype(BF16)
    for d in range(2):
        r = _sigmoid(jnp.dot(xcb, wa_ref[d], preferred_element_type=F32) + p[5 + d:6 + d])
        i = _sigmoid(jnp.dot(xcb, wx_ref[d], preferred_element_type=F32) + p[7 + d:8 + d])
        z = -p[9 + d:10 + d]
        softplus = jnp.maximum(z, 0.0) + jnp.log(1.0 + jnp.exp(-jnp.abs(z)))
        a = jnp.exp((-LRU_C) * softplus * r)
        a_ref[d] = a
        q = 1.0 - a * a
        b_ref[d] = (q * lax.rsqrt(jnp.maximum(q, 1e-30))) * (i * xc)

    grow = lax.broadcasted_iota(jnp.int32, (SUBLANE, hp), 0)
    ngroups = t // SUBLANE

    def group(g, carry):
        cf, cb = carry
        f0 = pl.multiple_of(g * SUBLANE, SUBLANE)
        af = a_ref[0, pl.ds(f0, SUBLANE), :]
        bf = b_ref[0, pl.ds(f0, SUBLANE), :]
        r0 = pl.multiple_of((ngroups - 1 - g) * SUBLANE, SUBLANE)
        ab = a_ref[1, pl.ds(r0, SUBLANE), :]
        bb = b_ref[1, pl.ds(r0, SUBLANE), :]
        for s in (1, 2, 4):
            mf = grow >= s
            bf = jnp.where(mf, af * pltpu.roll(bf, s, 0) + bf, bf)
            af = jnp.where(mf, af * pltpu.roll(af, s, 0), af)
            mb = grow < SUBLANE - s
            bb = jnp.where(mb, ab * pltpu.roll(bb, SUBLANE - s, 0) + bb, bb)
            ab = jnp.where(mb, ab * pltpu.roll(ab, SUBLANE - s, 0), ab)
        hf = af * cf + bf
        hb = ab * cb + bb
        hf_ref[pl.ds(f0, SUBLANE), :] = hf
        hb_ref[pl.ds(r0, SUBLANE), :] = hb
        cf = jnp.broadcast_to(hf[SUBLANE - 1:SUBLANE, :], (SUBLANE, hp))
        cb = jnp.broadcast_to(hb[0:1, :], (SUBLANE, hp))
        return cf, cb

    h0 = h0_ref[...]
    cf0 = jnp.broadcast_to(h0[0:1, :], (SUBLANE, hp))
    cb0 = jnp.broadcast_to(h0[1:2, :], (SUBLANE, hp))
    cf, cb = lax.fori_loop(0, ngroups, group, (cf0, cb0))
    fin_ref[0:1, :] = cf[0:1, :]
    fin_ref[1:2, :] = cb[0:1, :]

    y = y_ref[...]
    gelu = 0.5 * y * (1.0 + jnp.tanh(math.sqrt(2.0 / math.pi) * (y + 0.044715 * (y * y * y))))
    mixed_ref[...] = ((hf_ref[...] + hb_ref[...]) * gelu).astype(mixed_ref.dtype)


def _lru_core(yx, row0, n_seq, t, wa, wx, params, h0):
    nh, _, hp = params.shape
    dp = nh * hp
    off = row0 // t
    return pl.pallas_call(
        _lru_kernel,
        out_shape=(jax.ShapeDtypeStruct((n_seq * t, dp), BF16),
                   jax.ShapeDtypeStruct((n_seq, 2, dp), F32)),
        grid=(nh, n_seq),
        in_specs=[
            pl.BlockSpec((t, hp), lambda h, s: (s + off, h)),
            pl.BlockSpec((t, hp), lambda h, s: (s + off, h + nh)),
            pl.BlockSpec((2, None, hp, hp), lambda h, s: (0, h, 0, 0)),
            pl.BlockSpec((2, None, hp, hp), lambda h, s: (0, h, 0, 0)),
            pl.BlockSpec((None, 16, hp), lambda h, s: (h, 0, 0)),
            pl.BlockSpec((None, 2, hp), lambda h, s: (s, 0, h)),
        ],
        out_specs=(pl.BlockSpec((t, hp), lambda h, s: (s, h)),
                   pl.BlockSpec((None, 2, hp), lambda h, s: (s, 0, h))),
        scratch_shapes=[
            pltpu.VMEM((2, t, hp), F32), pltpu.VMEM((2, t, hp), F32),
            pltpu.VMEM((t, hp), F32), pltpu.VMEM((t, hp), F32),
        ],
        compiler_params=_cparams(2),
        name="lru_core_t%d" % t,
    )(yx, yx, wa, wx, params, h0)


def _router_kernel(z_ref, g_ref, b_ref, sh_ref, sc_ref, wh_ref, wl_ref, br_ref,
                   x_ref, xm_ref, ri_ref, rw_ref, cnt_ref, *, n_groups, n_experts):
    z = z_ref[...]
    mu = jnp.mean(z, axis=-1, keepdims=True)
    zc = z - mu
    var = jnp.mean(zc * zc, axis=-1, keepdims=True)
    x = zc * lax.rsqrt(var + LN_EPS) * g_ref[...] + b_ref[...]
    x_ref[...] = x
    xm = x * (1.0 + sc_ref[...]) + sh_ref[...]
    half = xm.shape[1] // 2
    xm_ref[...] = _pack_bf16_pair(xm[:, :half], xm[:, half:])
    xh = xm.astype(BF16)
    xl = (xm - xh.astype(F32)).astype(BF16)
    wh = wh_ref[...]
    logits = (jnp.dot(xh, wh, preferred_element_type=F32)
              + jnp.dot(xh, wl_ref[...], preferred_element_type=F32)
              + jnp.dot(xl, wh, preferred_element_type=F32)) + br_ref[...]
    tt = logits.shape[0]
    lane = lax.broadcasted_iota(jnp.int32, (tt, LANE), 1).astype(F32)
    neg = jnp.float32(-jnp.inf)
    big = jnp.float32(LANE)
    epg = n_experts // n_groups
    is_grp = (lane >= n_experts) & (lane < n_experts + n_groups)
    gl = jnp.where(is_grp, logits, neg)
    gmax = jnp.max(gl, axis=-1, keepdims=True)
    gsum = jnp.sum(jnp.where(is_grp, jnp.exp(gl - gmax), 0.0), axis=-1, keepdims=True)
    p_grp = 1.0 / gsum
    grp = jnp.min(jnp.where(gl == gmax, lane, big), axis=-1, keepdims=True) - n_experts
    in_grp = (lane >= grp * epg) & (lane < (grp + 1.0) * epg)
    el = jnp.where(in_grp, logits, neg)
    t1 = jnp.max(el, axis=-1, keepdims=True)
    i1 = jnp.min(jnp.where(el == t1, lane, big), axis=-1, keepdims=True)
    el2 = jnp.where(lane == i1, neg, el)
    t2 = jnp.max(el2, axis=-1, keepdims=True)
    i2 = jnp.min(jnp.where(el2 == t2, lane, big), axis=-1, keepdims=True)
    e2 = jnp.exp(t2 - t1)
    w1 = p_grp / (1.0 + e2)
    w2 = p_grp * e2 / (1.0 + e2)

    @pl.when(pl.program_id(0) == 0)
    def _():
        cnt_ref[...] = jnp.zeros_like(cnt_ref)

    sel1 = lane == i1
    sel2 = lane == i2
    onehot = jnp.where(sel1 | sel2, 1.0, 0.0)
    tr = lax.broadcasted_iota(jnp.int32, (tt, tt), 0)
    tc = lax.broadcasted_iota(jnp.int32, (tt, tt), 1)
    before = jnp.where(tc < tr, 1.0, 0.0).astype(BF16)
    prior = jnp.dot(before, onehot.astype(BF16), preferred_element_type=F32) + cnt_ref[...]
    r1 = jnp.sum(jnp.where(sel1, prior, 0.0), axis=-1, keepdims=True)
    r2 = jnp.sum(jnp.where(sel2, prior, 0.0), axis=-1, keepdims=True)
    cnt_ref[...] += jnp.sum(onehot, axis=0, keepdims=True)
    ri = jnp.where(lane == 0, i1, jnp.where(lane == 1, i2, jnp.where(lane == 2, r1, jnp.where(lane == 3, r2, 0.0))))
    ri_ref[...] = ri.astype(jnp.int32)
    rw_ref[...] = jnp.where(lane == 0, w1, jnp.where(lane == 1, w2, 0.0))


def _router(z, g, b, mods, k_shift, k_scale, wr_hi, wr_lo, br, mod_row, n_groups, n_experts):
    n, d = z.shape
    tt = ROW_TILE
    vec = lambda v: v.reshape(1, d)
    return pl.pallas_call(
        functools.partial(_router_kernel, n_groups=n_groups, n_experts=n_experts),
        out_shape=(jax.ShapeDtypeStruct((n, d), F32),
                   jax.ShapeDtypeStruct((n, d // 2), jnp.int32),
                   jax.ShapeDtypeStruct((n, LANE), jnp.int32),
                   jax.ShapeDtypeStruct((n, LANE), F32),
                   jax.ShapeDtypeStruct((1, LANE), F32)),
        grid=(n // tt,),
        in_specs=[
            pl.BlockSpec((tt, d), lambda i: (i, 0)),
            pl.BlockSpec((1, d), lambda i: (0, 0)),
            pl.BlockSpec((1, d), lambda i: (0, 0)),
            pl.BlockSpec((None, None, 1, d), lambda i: (k_shift, mod_row(i), 0, 0)),
            pl.BlockSpec((None, None, 1, d), lambda i: (k_scale, mod_row(i), 0, 0)),
            pl.BlockSpec((d, LANE), lambda i: (0, 0)),
            pl.BlockSpec((d, LANE), lambda i: (0, 0)),
            pl.BlockSpec((1, LANE), lambda i: (0, 0)),
        ],
        out_specs=(pl.BlockSpec((tt, d), lambda i: (i, 0)),
                   pl.BlockSpec((tt, d // 2), lambda i: (i, 0)),
                   pl.BlockSpec((tt, LANE), lambda i: (i, 0)),
                   pl.BlockSpec((tt, LANE), lambda i: (i, 0)),
                   pl.BlockSpec((1, LANE), lambda i: (0, 0))),
        compiler_params=_cparams(1),
        name="moe_router",
    )(z, vec(g), vec(b), mods, mods, wr_hi, wr_lo, br)


def _for_rows_spread(n_rows, fn):
    n_tiles = n_rows // SUBLANE
    for q in range(SUBLANE):
        def body(t, carry, q=q):
            fn(t * SUBLANE + q)
            return carry
        lax.fori_loop(0, n_tiles, body, 0, unroll=DMA_ISSUE_UNROLL)


def _gather_kernel(nblk_ref, idx_ref, idx_next_ref, x_hbm, o_ref, buf_ref, sem):
    b = pl.program_id(0)
    nblk = nblk_ref[0]
    tb = idx_ref.shape[-1]
    slot = b % 2

    def issue(ids_ref, dst_slot):
        def copy_row(r):
            pltpu.make_async_copy(x_hbm.at[pl.ds(ids_ref[0, 0, r], 1)],
                                  buf_ref.at[dst_slot, pl.ds(r, 1)], sem.at[dst_slot]).start()
        _for_rows_spread(tb, copy_row)

    @pl.when(b == 0)
    def _():
        issue(idx_ref, 0)

    @pl.when(b + 1 < nblk)
    def _():
        issue(idx_next_ref, 1 - slot)

    @pl.when(b < nblk)
    def _():
        pltpu.make_async_copy(x_hbm.at[pl.ds(0, tb)], buf_ref.at[slot], sem.at[slot]).wait()
        w = buf_ref[slot]
        half = w.shape[1]
        o_ref[:, :half] = _unpack_hi(w).astype(o_ref.dtype)
        o_ref[:, half:] = _unpack_lo(w).astype(o_ref.dtype)

    @pl.when(b >= nblk)
    def _():
        o_ref[...] = jnp.zeros_like(o_ref)


def _gather_rows(x, idx, n_blocks):
    nb, _, tb = idx.shape
    d = 2 * x.shape[1]
    return pl.pallas_call(
        _gather_kernel,
        out_shape=jax.ShapeDtypeStruct((nb * tb, d), BF16),
        grid=(nb,),
        in_specs=[
            pl.BlockSpec(memory_space=pltpu.SMEM),
            pl.BlockSpec((1, 1, tb), lambda b: (b, 0, 0), memory_space=pltpu.SMEM),
            pl.BlockSpec((1, 1, tb), lambda b: (jnp.minimum(b + 1, nb - 1), 0, 0), memory_space=pltpu.SMEM),
            pl.BlockSpec(memory_space=pl.ANY),
        ],
        out_specs=pl.BlockSpec((tb, d), lambda b: (b, 0)),
        scratch_shapes=[pltpu.VMEM((2, tb, d // 2), x.dtype), pltpu.SemaphoreType.DMA((2,))],
        compiler_params=_cparams(1),
        name="moe_gather",
    )(n_blocks.reshape(1).astype(jnp.int32), idx, idx, x)


def _moe_up_kernel(sb_ref, ob_ref, oc_ref, snew_ref, sci_ref, nvalid_ref, ce_ref, cc_ref, nchg_ref,
                   xs_ref, w1_hbm, w3_hbm, h_ref, st1_ref, st3_ref, w1c_ref, w3c_ref, sem, *, layer):
    s = pl.program_id(0)
    fc = st1_ref.shape[1]

    def fetch(i):
        c0 = pl.multiple_of(cc_ref[i] * fc, fc)
        return (pltpu.make_async_copy(w1_hbm.at[layer, ce_ref[i], :, pl.ds(c0, fc)], st1_ref, sem.at[0]),
                pltpu.make_async_copy(w3_hbm.at[layer, ce_ref[i], :, pl.ds(c0, fc)], st3_ref, sem.at[1]))

    @pl.when(s == 0)
    def _():
        for cp in fetch(0):
            cp.start()

    @pl.when(snew_ref[s] == 1)
    def _():
        i = sci_ref[s]
        for cp in fetch(i):
            cp.wait()
        w1c_ref[...] = st1_ref[...].astype(BF16)
        w3c_ref[...] = st3_ref[...].astype(BF16)

        @pl.when(i + 1 < nchg_ref[0])
        def _():
            for cp in fetch(i + 1):
                cp.start()

    @pl.when(s < nvalid_ref[0])
    def _():
        xb = xs_ref[...]
        h1 = jnp.dot(xb, w1c_ref[...], preferred_element_type=F32)
        h3 = jnp.dot(xb, w3c_ref[...], preferred_element_type=F32)
        h_ref[...] = (h1 * _sigmoid(h1) * h3).astype(h_ref.dtype)

    @pl.when(s >= nvalid_ref[0])
    def _():
        h_ref[...] = jnp.zeros_like(h_ref)


def _moe_up(tables, xs, w1, w3, layer, fc):
    slots, d = xs.shape
    f = w1.shape[3]
    t = tables
    prefetch = (t["sb"], t["ob"], t["oc"], t["snew"], t["sci"], t["nvalid"], t["ce"], t["cc"], t["nchg"])
    nsteps = t["sb"].shape[0]
    return pl.pallas_call(
        functools.partial(_moe_up_kernel, layer=layer),
        out_shape=jax.ShapeDtypeStruct((slots, f), BF16),
        grid_spec=pltpu.PrefetchScalarGridSpec(
            num_scalar_prefetch=len(prefetch),
            grid=(nsteps,),
            in_specs=[
                pl.BlockSpec((MOE_TB, d), lambda s, sb, *_: (sb[s], 0)),
                pl.BlockSpec(memory_space=pl.ANY),
                pl.BlockSpec(memory_space=pl.ANY),
            ],
            out_specs=pl.BlockSpec((MOE_TB, fc), lambda s, sb, ob, oc, *_: (ob[s], oc[s])),
            scratch_shapes=[pltpu.VMEM((d, fc), F32), pltpu.VMEM((d, fc), F32),
                            pltpu.VMEM((d, fc), BF16), pltpu.VMEM((d, fc), BF16),
                            pltpu.SemaphoreType.DMA((2,))],
        ),
        compiler_params=_cparams(1),
        name="moe_up",
    )(*prefetch, xs, w1, w3)


def _moe_down_kernel(sb_ref, ob_ref, snew_ref, sci_ref, nvalid_ref, ce_ref, nchg_ref,
                     h_ref, w2_hbm, y_ref, st_ref, w2c_ref, sem, *, layer):
    s = pl.program_id(0)

    def fetch(i):
        return pltpu.make_async_copy(w2_hbm.at[layer, ce_ref[i]], st_ref, sem.at[0])

    @pl.when(s == 0)
    def _():
        fetch(0).start()

    @pl.when(snew_ref[s] == 1)
    def _():
        i = sci_ref[s]
        fetch(i).wait()
        w2c_ref[...] = st_ref[...].astype(BF16)

        @pl.when(i + 1 < nchg_ref[0])
        def _():
            fetch(i + 1).start()

    @pl.when(s < nvalid_ref[0])
    def _():
        y = jnp.dot(h_ref[...], w2c_ref[...], preferred_element_type=F32)
        half = y.shape[1] // 2
        y_ref[...] = _pack_bf16_pair(y[:, :half], y[:, half:])

    @pl.when(s >= nvalid_ref[0])
    def _():
        y_ref[...] = jnp.zeros_like(y_ref)


def _moe_down(tables, h, w2, layer):
    slots, f = h.shape
    d = w2.shape[3]
    t = tables
    prefetch = (t["sb"], t["ob"], t["snew"], t["sci"], t["nvalid"], t["ce"], t["nchg"])
    nsteps = t["sb"].shape[0]
    return pl.pallas_call(
        functools.partial(_moe_down_kernel, layer=layer),
        out_shape=jax.ShapeDtypeStruct((slots, d // 2), jnp.int32),
        grid_spec=pltpu.PrefetchScalarGridSpec(
            num_scalar_prefetch=len(prefetch),
            grid=(nsteps,),
            in_specs=[
                pl.BlockSpec((MOE_TB, f), lambda s, sb, *_: (sb[s], 0)),
                pl.BlockSpec(memory_space=pl.ANY),
            ],
            out_specs=pl.BlockSpec((MOE_TB, d // 2), lambda s, sb, ob, *_: (ob[s], 0)),
            scratch_shapes=[pltpu.VMEM((f, d), F32), pltpu.VMEM((f, d), BF16), pltpu.SemaphoreType.DMA((1,))],
        ),
        compiler_params=_cparams(1),
        name="moe_down",
    )(*prefetch, h, w2)


def _combine_kernel(dest_ref, dest_next_ref, y_hbm, x_ref, rw_ref, gate_ref, g_ref, b_ref, o_ref, ybuf_ref, sem, *, alpha):
    i = pl.program_id(0)
    nt = pl.num_programs(0)
    tt = x_ref.shape[0]
    slot = i % 2

    def issue(ids_ref, dst_slot):
        def copy_rows(r):
            for k in range(TOP_K):
                pltpu.make_async_copy(y_hbm.at[pl.ds(ids_ref[0, 0, k * tt + r], 1)],
                                      ybuf_ref.at[dst_slot, k, pl.ds(r, 1)], sem.at[dst_slot]).start()
        _for_rows_spread(tt, copy_rows)

    @pl.when(i == 0)
    def _():
        issue(dest_ref, 0)

    @pl.when(i + 1 < nt)
    def _():
        issue(dest_next_ref, 1 - slot)

    for k in range(TOP_K):
        pltpu.make_async_copy(y_hbm.at[pl.ds(0, tt)], ybuf_ref.at[slot, k], sem.at[slot]).wait()

    rows = 32

    def row_block(c, carry):
        r0 = pl.multiple_of(c * rows, rows)
        rw = rw_ref[pl.ds(r0, rows), :]
        y0 = ybuf_ref[slot, 0, pl.ds(r0, rows), :]
        y1 = ybuf_ref[slot, 1, pl.ds(r0, rows), :]
        f = jnp.concatenate([rw[:, 0:1] * _unpack_hi(y0) + rw[:, 1:2] * _unpack_hi(y1),
                             rw[:, 0:1] * _unpack_lo(y0) + rw[:, 1:2] * _unpack_lo(y1)], axis=1)
        z = alpha * x_ref[pl.ds(r0, rows), :] + gate_ref[...] * f
        mu = jnp.mean(z, axis=-1, keepdims=True)
        zc = z - mu
        var = jnp.mean(zc * zc, axis=-1, keepdims=True)
        o_ref[pl.ds(r0, rows), :] = zc * lax.rsqrt(var + LN_EPS) * g_ref[...] + b_ref[...]
        return carry

    lax.fori_loop(0, tt // rows, row_block, 0)


def _combine(dest, y, x, rw, mods, k_gate, g, b, mod_row, alpha, tile0, ntiles):
    d = x.shape[1]
    tt = ROW_TILE
    vec = lambda v: v.reshape(1, d)
    return pl.pallas_call(
        functools.partial(_combine_kernel, alpha=alpha),
        out_shape=jax.ShapeDtypeStruct((ntiles * tt, d), F32),
        grid=(ntiles,),
        in_specs=[
            pl.BlockSpec((1, 1, TOP_K * tt), lambda i: (i + tile0, 0, 0), memory_space=pltpu.SMEM),
            pl.BlockSpec((1, 1, TOP_K * tt), lambda i: (jnp.minimum(i + 1, ntiles - 1) + tile0, 0, 0),
                         memory_space=pltpu.SMEM),
            pl.BlockSpec(memory_space=pl.ANY),
            pl.BlockSpec((tt, d), lambda i: (i + tile0, 0)),
            pl.BlockSpec((tt, LANE), lambda i: (i + tile0, 0)),
            pl.BlockSpec((None, None, 1, d), lambda i: (k_gate, mod_row(i + tile0), 0, 0)),
            pl.BlockSpec((1, d), lambda i: (0, 0)),
            pl.BlockSpec((1, d), lambda i: (0, 0)),
        ],
        out_specs=pl.BlockSpec((tt, d), lambda i: (i, 0)),
        scratch_shapes=[pltpu.VMEM((2, TOP_K, tt, d // 2), jnp.int32), pltpu.SemaphoreType.DMA((2,))],
        compiler_params=_cparams(1),
        name="moe_combine_ln",
    )(dest, dest, y, x, rw, mods, vec(g), vec(b))


def _dispatch_tables(flat_e, flat_rank, counts, n_chunks_up, n_chunks_down):
    nk = flat_e.shape[0]
    n_experts = counts.shape[0]
    tb = MOE_TB
    nb = nk // tb + n_experts
    eye = (flat_e[:, None] == jnp.arange(n_experts, dtype=jnp.int32)[None, :])
    blocks_e = (counts + tb - 1) // tb
    blk_end = jnp.cumsum(blocks_e)
    blk_start = blk_end - blocks_e
    dest = jnp.sum(jnp.where(eye, blk_start[None, :] * tb, 0), axis=1) + flat_rank
    slot_tok = jnp.zeros((nb * tb,), jnp.int32).at[dest].set(jnp.arange(nk, dtype=jnp.int32) // TOP_K)
    n_blocks = blk_end[-1]

    def steps(n_chunks):
        ns = nb * n_chunks
        per_e = blocks_e * n_chunks
        e_end = jnp.cumsum(per_e)
        e_start = e_end - per_e
        s = jnp.arange(ns, dtype=jnp.int32)
        nvalid = e_end[-1]
        sv = jnp.minimum(s, nvalid - 1)
        e = jnp.minimum(jnp.sum((sv[:, None] >= e_end[None, :]).astype(jnp.int32), axis=1), n_experts - 1)
        pick = e[:, None] == jnp.arange(n_experts, dtype=jnp.int32)[None, :]
        local = sv - jnp.sum(jnp.where(pick, e_start[None, :], 0), axis=1)
        be = jnp.maximum(jnp.sum(jnp.where(pick, blocks_e[None, :], 0), axis=1), 1)
        chunk = (local // be).astype(jnp.int32)
        blk = (jnp.sum(jnp.where(pick, blk_start[None, :], 0), axis=1) + local % be).astype(jnp.int32)
        key = e * n_chunks + chunk
        new = jnp.concatenate([jnp.ones((1,), jnp.int32), (key[1:] != key[:-1]).astype(jnp.int32)])
        extra = s - nvalid
        out_blk = jnp.where(s < nvalid, blk, n_blocks + extra // n_chunks).astype(jnp.int32)
        out_chunk = jnp.where(s < nvalid, chunk, extra % n_chunks).astype(jnp.int32)
        n_pairs = n_experts * n_chunks
        pair = jnp.arange(n_pairs, dtype=jnp.int32)
        live = jnp.repeat(blocks_e > 0, n_chunks)
        pos = jnp.where(live, jnp.cumsum(live.astype(jnp.int32)) - 1, n_pairs)
        ce = jnp.zeros((n_pairs,), jnp.int32).at[pos].set(pair // n_chunks, mode='drop')
        cc = jnp.zeros((n_pairs,), jnp.int32).at[pos].set(pair % n_chunks, mode='drop')
        as1 = lambda v: v.reshape(1).astype(jnp.int32)
        return dict(sb=blk, ob=out_blk, oc=out_chunk, snew=new, sci=(jnp.cumsum(new) - 1).astype(jnp.int32),
                    nvalid=as1(nvalid), ce=ce, cc=cc, nchg=as1(jnp.sum(live)))

    return dest, slot_tok.reshape(nb, 1, tb), steps(n_chunks_up), steps(n_chunks_down), n_blocks


def _hier_moe_block(z, mods, ln1_g, ln1_b, ln2_g, ln2_b, rg_w, rg_b, re_w, re_b, w1, w3, w2, layer, mod_row, alpha,
                    out_splits):
    n, d = z.shape
    n_groups = rg_w.shape[1]
    n_experts = re_w.shape[1]
    f = w1.shape[3]
    wr = jnp.zeros((d, LANE), F32).at[:, :n_experts].set(re_w).at[:, n_experts:n_experts + n_groups].set(rg_w)
    br = jnp.zeros((1, LANE), F32).at[0, :n_experts].set(re_b).at[0, n_experts:n_experts + n_groups].set(rg_b)
    wr_hi = wr.astype(BF16)
    wr_lo = (wr - wr_hi.astype(F32)).astype(BF16)
    x, xm, ri, rw, cnt = _router(z, ln1_g, ln1_b, mods, 3, 4, wr_hi, wr_lo, br, mod_row, n_groups, n_experts)
    flat_e = ri[:, :TOP_K].reshape(-1)
    flat_rank = ri[:, TOP_K:2 * TOP_K].reshape(-1)
    counts = cnt[0, :n_experts].astype(jnp.int32)
    fc = min(512, f)
    dest, slot_tok, up_tables, down_tables, n_blocks = _dispatch_tables(flat_e, flat_rank, counts, f // fc, 1)
    xs = _gather_rows(xm, slot_tok, n_blocks)
    h = _moe_up(up_tables, xs, w1, w3, layer, fc)
    y = _moe_down(down_tables, h, w2, layer)
    tt = ROW_TILE
    dest_t = dest.reshape(n // tt, tt, TOP_K).transpose(0, 2, 1).reshape(n // tt, 1, TOP_K * tt)
    return [_combine(dest_t, y, x, rw, mods, 5, ln2_g, ln2_b, mod_row, alpha, t0, nt) for t0, nt in out_splits]


def _pad_heads(v, nh, hp):
    lead = v.shape[:-1]
    hb = v.shape[-1] // nh
    v = v.reshape(lead + (nh, hb))
    v = jnp.pad(v, [(0, 0)] * len(lead) + [(0, 0), (0, hp - hb)])
    return v.reshape(lead + (nh * hp,))


def kernel(x_prompt, x_sample, state_lru, c, c_ctx, mod_w, mod_b, ln_g, ln_b, conv_w_in, conv_b_in, conv_w_dw, conv_b_dw, conv_ln_g, conv_ln_b, conv_w_out, conv_b_out, lru_w_in, lru_b_in, lru_w_sc, lru_b_sc, lru_w_a, lru_b_a, lru_w_x, lru_b_x, lru_lam, lru_w_out, lru_b_out, moe_rg_w, moe_rg_b, moe_re_w, moe_re_b, moe_w1, moe_w3, moe_w2):
    batch, seq, d = x_prompt.shape
    dec_batch, dec_seq, _ = x_sample.shape
    depth = mod_w.shape[0]
    n_p = batch * seq
    n_s = dec_batch * dec_seq
    alpha = (2 * depth) ** 0.25
    assert n_p % 512 == 0 and dec_seq % 512 == 0 and seq % ROW_TILE == 0 and ROW_TILE % GRID_W == 0

    x_parts = [x_prompt.reshape(n_p, d), x_sample.reshape(n_s, d)]

    n_rows = _round_up(dec_batch + 1, SUBLANE)
    cond = jnp.zeros((n_rows, d), F32).at[:dec_batch].set(c).at[dec_batch].set(c_ctx)
    m_all = _adaln(cond, mod_w, mod_b)
    m_all = m_all.reshape(depth, n_rows, 6, 1, d).transpose(0, 2, 1, 3, 4)
    mod_row = _mod_row_fn(ROW_TILE, n_p, dec_seq, dec_batch)

    new_states = []
    for l in range(depth):
        j = l // 2
        mods = m_all[l]
        if l % 2 == 0:
            u = _inproj(x_parts, mods, 0, 1, conv_w_in[j].astype(BF16), conv_b_in[j], mod_row, glu=True)
            dw = (conv_w_dw[j], conv_b_dw[j], conv_ln_g[j], conv_ln_b[j])
            act_parts = [_dwconv(u, 0, n_p, seq, *dw), _dwconv(u, n_p, n_s, GRID_W, *dw)]
            w_out = conv_w_out[j].astype(BF16)
            b_out = conv_b_out[j]
        else:
            nh = lru_w_a.shape[2]
            d_rnn = lru_w_sc.shape[2]
            hp = _round_up(d_rnn // nh, LANE)
            ph = functools.partial(_pad_heads, nh=nh, hp=hp)
            w_in_p = _pad_heads(lru_w_in[j].astype(BF16), 2 * nh, hp)
            b_in_p = _pad_heads(lru_b_in[j], 2 * nh, hp)
            yx = _inproj(x_parts, mods, 0, 1, w_in_p, b_in_p, mod_row, glu=False, tn=1024)

            def pad_sq(w):
                hb = w.shape[-1]
                return jnp.pad(w, [(0, 0), (0, 0), (0, hp - hb), (0, hp - hb)]).astype(BF16)

            rows = jnp.concatenate([
                ph(lru_w_sc[j]), ph(lru_b_sc[j])[None], ph(lru_b_a[j]), ph(lru_b_x[j]), ph(lru_lam[j])], axis=0)
            params = jnp.zeros((16, nh * hp), F32).at[:rows.shape[0]].set(rows)
            params = params.reshape(16, nh, hp).transpose(1, 0, 2)
            wa = pad_sq(lru_w_a[j])
            wx = pad_sq(lru_w_x[j])
            h0_p = jnp.zeros((batch, 2, nh * hp), F32)
            h0_s = ph(state_lru[:, j])
            mixed_p, fin_p = _lru_core(yx, 0, batch, seq, wa, wx, params, h0_p)
            mixed_s, _ = _lru_core(yx, n_p, dec_batch, dec_seq, wa, wx, params, h0_s)
            act_parts = [mixed_p, mixed_s]
            hb = d_rnn // nh
            new_states.append(fin_p.reshape(batch, 2, nh, hp)[..., :hb].reshape(batch, 2, d_rnn))
            w_out = jnp.pad(lru_w_out[j].reshape(nh, hb, d), [(0, 0), (0, hp - hb), (0, 0)]).reshape(nh * hp, d).astype(BF16)
            b_out = lru_b_out[j]
        z = _outproj(act_parts, w_out, b_out, x_parts, mods, 2, mod_row, alpha)
        tiles_p, tiles_s = n_p // ROW_TILE, n_s // ROW_TILE
        splits = [(0, tiles_p), (tiles_p, tiles_s)] if l == depth - 1 else [(0, tiles_p + tiles_s)]
        x_parts = _hier_moe_block(z, mods, ln_g[l, 0], ln_b[l, 0], ln_g[l, 1], ln_b[l, 1],
                                  moe_rg_w[l], moe_rg_b[l], moe_re_w[l], moe_re_b[l],
                                  moe_w1, moe_w3, moe_w2, l, mod_row, alpha, splits)

    y_prompt = x_parts[0].reshape(batch, seq, d)
    y_sample = x_parts[1].reshape(dec_batch, dec_seq, d)
    new_state_lru = jnp.stack(new_states, axis=1).astype(x_prompt.dtype)
    return (y_prompt, y_sample, new_state_lru)
```
